```python
import math
import jax, jax.numpy as jnp
from jax import lax
import numpy as np

D_MODEL = 1024
BATCH = 1
SEQ = 16384
DEPTH = 1
DEC_BATCH = 32
DEC_SEQ = 4
PAST_LEN = 16384
PAGE_SIZE = 128

N_META = 16
D_MIX = D_MODEL
D_ATTN = D_MIX // 2
D_POOL = D_MIX - D_ATTN
HEAD_DIM = 64
N_HEADS = D_ATTN // (2 * HEAD_DIM)
V_DIM = 2 * HEAD_DIM
POOL_WINDOWS = (2, 4, 8, 16)
N_POOL_GROUPS = len(POOL_WINDOWS)
POOL_GROUP = D_POOL // N_POOL_GROUPS
POOL_STATE = max(POOL_WINDOWS) - 1
D_IN = 3 * D_ATTN + D_POOL + D_MIX
ROPE_THETA = 10000.0
RMS_EPS = 1e-6
SUBLN_EPS = 1e-5
Q_BLOCK = 128

kernel_name = 'hymba_diffattn_pool_decode_step'

F32 = jnp.float32


def rmsnorm(x, gain, eps):
    xf = x.astype(F32)
    y = xf * lax.rsqrt(jnp.mean(xf * xf, axis=-1, keepdims=True) + eps)
    return (y * gain.astype(F32)).astype(x.dtype)


def rope(x, pos):
    half = HEAD_DIM // 2
    inv_freq = ROPE_THETA ** (-jnp.arange(0, HEAD_DIM, 2, dtype=F32) / HEAD_DIM)
    ang = pos.astype(F32)[:, None] * inv_freq[None, :]
    cos = jnp.cos(ang)[:, None, None, :]
    sin = jnp.sin(ang)[:, None, None, :]
    xf = x.astype(F32)
    x1, x2 = xf[..., :half], xf[..., half:]
    out = jnp.concatenate([x1 * cos - x2 * sin, x2 * cos + x1 * sin], axis=-1)
    return out.astype(x.dtype)


def project(h, pos, norm_gain, w_in):
    b, l = h.shape[:2]
    z = rmsnorm(h, norm_gain, RMS_EPS) @ w_in
    q = rope(z[..., :D_ATTN].reshape(b, l, N_HEADS, 2, HEAD_DIM), pos)
    k = rope(z[..., D_ATTN:2 * D_ATTN].reshape(b, l, N_HEADS, 2, HEAD_DIM), pos)
    v = z[..., 2 * D_ATTN:3 * D_ATTN].reshape(b, l, N_HEADS, V_DIM)
    u = z[..., 3 * D_ATTN:3 * D_ATTN + D_POOL]
    gate = z[..., 3 * D_ATTN + D_POOL:]
    return q, k, v, u, gate


def diff_weights(q, k, mask, lam):
    s = jnp.einsum('bqhcd,bkhcd->bhcqk', q, k, preferred_element_type=F32) * (HEAD_DIM ** -0.5)
    s = jnp.where(mask, s, -jnp.inf)
    p = jax.nn.softmax(s, axis=-1)
    return p[:, :, 0] - lam * p[:, :, 1]


def diff_out(w, v, subln_gain, lam_init):
    o = jnp.einsum('bhqk,bkhe->bqhe', w, v.astype(F32))
    return rmsnorm(o, subln_gain, SUBLN_EPS) * (1.0 - lam_init)


def prompt_diff_attention(q, k, v, lam, subln_gain, lam_init):
    b, l = q.shape[:2]
    n_blk = -(-l // Q_BLOCK)
    lp = n_blk * Q_BLOCK
    pad5 = ((0, 0), (0, lp - l), (0, 0), (0, 0), (0, 0))
    qp = jnp.pad(q, pad5)
    kp = jnp.pad(k, pad5)
    vp = jnp.pad(v, pad5[:4])
    q_blocks = jnp.moveaxis(qp.reshape(b, n_blk, Q_BLOCK, N_HEADS, 2, HEAD_DIM), 1, 0)
    k_pos = jnp.arange(lp)

    def one_block(args):
        qb, start = args
        q_pos = start + jnp.arange(Q_BLOCK)
        mask = k_pos[None, :] <= q_pos[:, None]
        return diff_out(diff_weights(qb, kp, mask, lam), vp, subln_gain, lam_init)

    o = lax.map(one_block, (q_blocks, jnp.arange(n_blk) * Q_BLOCK))
    return jnp.moveaxis(o, 0, 1).reshape(b, lp, N_HEADS, V_DIM)[:, :l]


def sample_diff_attention(q, k_new, v_new, k_past, v_past, lam, subln_gain, lam_init):
    s_len = q.shape[1]
    p_len = k_past.shape[1]
    k = jnp.concatenate([k_past, k_new.astype(k_past.dtype)], axis=1)
    v = jnp.concatenate([v_past, v_new.astype(v_past.dtype)], axis=1)
    k_idx = jnp.arange(p_len + s_len)
    q_idx = p_len + jnp.arange(s_len)
    mask = k_idx[None, :] <= q_idx[:, None]
    return diff_out(diff_weights(q, k, mask, lam), v, subln_gain, lam_init)


def pool_mix(u, pos, w_pool, pool_scale):
    b, l = u.shape[:2]
    uf = u.astype(F32)
    csum = jnp.concatenate([jnp.zeros((b, 1, D_POOL), F32), jnp.cumsum(uf, axis=1)], axis=1)
    means = []
    for g, w in enumerate(POOL_WINDOWS):
        cg = csum[..., g * POOL_GROUP:(g + 1) * POOL_GROUP]
        lower = jnp.pad(cg, ((0, 0), (w - 1, 0), (0, 0)))[:, :l]
        count = jnp.minimum(pos + 1, w).astype(F32)[None, :, None]
        means.append((cg[:, 1:] - lower) / count)
    mean = jnp.stack(means, axis=2)
    d = mean - uf.reshape(b, l, N_POOL_GROUPS, POOL_GROUP)
    y = jnp.einsum('blgc,gce->blge', d, w_pool.astype(F32)).reshape(b, l, D_POOL)
    return y * pool_scale.astype(F32)


def merge(h, attn_o, pool_o, gate, w_out):
    b, l = h.shape[:2]
    mixed = jnp.concatenate([attn_o.reshape(b, l, D_ATTN), pool_o], axis=-1)
    mixed = mixed * jax.nn.silu(gate.astype(F32))
    return h + (mixed.astype(h.dtype) @ w_out)


def setup_inputs(seed: int = 0) -> dict:
    key = jax.random.key(seed)
    ks = jax.random.split(key, 20)
    n_pages = PAST_LEN // PAGE_SIZE
    n_pool_pages = (DEC_BATCH * n_pages * 5) // 4
    x_prompt = jax.random.normal(ks[0], (BATCH, SEQ, D_MODEL), F32)
    x_sample = jax.random.normal(ks[1], (DEC_BATCH, DEC_SEQ, D_MODEL), F32)
    cache_k = jax.random.normal(ks[2], (DEPTH, n_pool_pages, PAGE_SIZE, N_HEADS, 2 * HEAD_DIM), F32)
    cache_v = jax.random.normal(ks[3], (DEPTH, n_pool_pages, PAGE_SIZE, N_HEADS, V_DIM), F32)
    state_pool = jax.random.normal(ks[4], (DEPTH, DEC_BATCH, POOL_STATE, D_POOL), F32)
    perm = jax.random.permutation(ks[5], n_pool_pages)[:DEC_BATCH * n_pages]
    page_table = perm.reshape(DEC_BATCH, n_pages).astype(jnp.int32)
    meta_tokens = jax.random.normal(ks[6], (N_META, D_MODEL), F32)
    norm_gain = 1.0 + 0.02 * jax.random.normal(ks[7], (DEPTH, D_MODEL), F32)
    w_in = jax.random.normal(ks[8], (DEPTH, D_MODEL, D_IN), F32) * D_MODEL ** -0.5
    lambda_q1 = 0.1 * jax.random.normal(ks[9], (DEPTH, HEAD_DIM), F32)
    lambda_k1 = 0.1 * jax.random.normal(ks[10], (DEPTH, HEAD_DIM), F32)
    lambda_q2 = 0.1 * jax.random.normal(ks[11], (DEPTH, HEAD_DIM), F32)
    lambda_k2 = 0.1 * jax.random.normal(ks[12], (DEPTH, HEAD_DIM), F32)
    subln_gain = 1.0 + 0.02 * jax.random.normal(ks[13], (DEPTH, V_DIM), F32)
    w_pool = jax.random.normal(ks[14], (DEPTH, N_POOL_GROUPS, POOL_GROUP, POOL_GROUP), F32) * POOL_GROUP ** -0.5
    pool_scale = 1.0 + 0.02 * jax.random.normal(ks[15], (DEPTH, D_POOL), F32)
    w_out = jax.random.normal(ks[16], (DEPTH, D_MIX, D_MODEL), F32) * D_MIX ** -0.5
    final_gain = 1.0 + 0.02 * jax.random.normal(ks[17], (D_MODEL,), F32)
    return {'x_prompt': x_prompt, 'x_sample': x_sample, 'cache_k': cache_k, 'cache_v': cache_v,
            'state_pool': state_pool, 'page_table': page_table, 'meta_tokens': meta_tokens,
            'norm_gain': norm_gain, 'w_in': w_in, 'lambda_q1': lambda_q1, 'lambda_k1': lambda_k1,
            'lambda_q2': lambda_q2, 'lambda_k2': lambda_k2, 'subln_gain': subln_gain,
            'w_pool': w_pool, 'pool_scale': pool_scale, 'w_out': w_out, 'final_gain': final_gain}


def reference(x_prompt, x_sample, cache_k, cache_v, state_pool, page_table, meta_tokens,
              norm_gain, w_in, lambda_q1, lambda_k1, lambda_q2, lambda_k2, subln_gain,
              w_pool, pool_scale, w_out, final_gain):
    b = x_prompt.shape[0]
    db, s_len = x_sample.shape[:2]
    meta = jnp.broadcast_to(meta_tokens.astype(x_prompt.dtype)[None], (b, N_META, D_MODEL))
    h_p = jnp.concatenate([meta, x_prompt], axis=1)
    h_s = x_sample
    l_p = h_p.shape[1]
    p_len = page_table.shape[1] * PAGE_SIZE
    pos_p = jnp.arange(l_p)
    pos_s = p_len + jnp.arange(s_len)
    pool_pos_s = p_len - POOL_STATE + jnp.arange(POOL_STATE + s_len)

    k_p_rows, v_p_rows, pool_p_rows = [], [], []
    k_s_rows, v_s_rows, pool_s_rows = [], [], []
    for layer in range(DEPTH):
        lam_init = 0.8 - 0.6 * math.exp(-0.3 * layer)
        lam = (jnp.exp(jnp.sum(lambda_q1[layer].astype(F32) * lambda_k1[layer].astype(F32)))
               - jnp.exp(jnp.sum(lambda_q2[layer].astype(F32) * lambda_k2[layer].astype(F32)))
               + lam_init)
        q_p, k_p, v_p, u_p, g_p = project(h_p, pos_p, norm_gain[layer], w_in[layer])
        q_s, k_s, v_s, u_s, g_s = project(h_s, pos_s, norm_gain[layer], w_in[layer])

        attn_p = prompt_diff_attention(q_p, k_p, v_p, lam, subln_gain[layer], lam_init)
        k_past = cache_k[layer, page_table].reshape(db, p_len, N_HEADS, 2, HEAD_DIM)
        v_past = cache_v[layer, page_table].reshape(db, p_len, N_HEADS, V_DIM)
        attn_s = sample_diff_attention(q_s, k_s, v_s, k_past, v_past, lam, subln_gain[layer], lam_init)

        pool_p = pool_mix(u_p, pos_p, w_pool[layer], pool_scale[layer])
        u_ext = jnp.concatenate([state_pool[layer].astype(u_s.dtype), u_s], axis=1)
        pool_s = pool_mix(u_ext, pool_pos_s, w_pool[layer], pool_scale[layer])[:, POOL_STATE:]

        h_p = merge(h_p, attn_p, pool_p, g_p, w_out[layer])
        h_s = merge(h_s, attn_s, pool_s, g_s, w_out[layer])

        k_p_rows.append(k_p.reshape(b, l_p, N_HEADS, 2 * HEAD_DIM))
        v_p_rows.append(v_p)
        pool_p_rows.append(u_p[:, -POOL_STATE:])
        k_s_rows.append(k_s.reshape(db, s_len, N_HEADS, 2 * HEAD_DIM))
        v_s_rows.append(v_s)
        pool_s_rows.append(u_ext[:, -POOL_STATE:])

    y_prompt = rmsnorm(h_p, final_gain, RMS_EPS)[:, N_META:]
    y_sample = rmsnorm(h_s, final_gain, RMS_EPS)
    new_k_prompt = jnp.stack(k_p_rows)
    new_v_prompt = jnp.stack(v_p_rows)
    new_pool_prompt = jnp.stack(pool_p_rows)
    new_k_sample = jnp.stack(k_s_rows)
    new_v_sample = jnp.stack(v_s_rows)
    new_pool_sample = jnp.stack(pool_s_rows)
    return (y_prompt, y_sample, new_k_prompt, new_v_prompt, new_pool_prompt, new_k_sample, new_v_sample, new_pool_sample)
```

```python
import functools
import math

import jax
import jax.numpy as jnp
from jax import lax
from jax.experimental import pallas as pl
from jax.experimental.pallas import tpu as pltpu

F32 = jnp.float32
BF16 = jnp.bfloat16

D_MODEL = 1024
N_META = 16
D_ATTN = 512
D_POOL = 512
HEAD_DIM = 64
N_HEADS = 4
V_DIM = 128
POOL_WINDOWS = (2, 4, 8, 16)
POOL_GROUP = 128
POOL_STATE = 15
D_IN = 3 * D_ATTN + D_POOL + D_MODEL
ROPE_THETA = 10000.0
RMS_EPS = 1e-6
SUBLN_EPS = 1e-5
PAGE_SIZE = 128
LANES = 128
HALO = 16
NEG_BIG = -1e30

PROJ_ROWS = 512
ATTN_TQ = 256
ATTN_TK = 512
MERGE_ROWS = 512
PAGES_PER_STEP = 8
VMEM_LIMIT = 48 * 1024 * 1024


def _dot(a, b):
    return jnp.dot(a, b, preferred_element_type=F32)


def _dot_nt(a, b):
    return lax.dot_general(a, b, (((1,), (1,)), ((), ())), preferred_element_type=F32)


def _lam_init(layer):
    return 0.8 - 0.6 * math.exp(-0.3 * layer)


def _lam(lq1, lk1, lq2, lk2):
    a = jnp.sum(lq1 * lk1, axis=1, keepdims=True)
    b = jnp.sum(lq2 * lk2, axis=1, keepdims=True)
    return jnp.exp(a) - jnp.exp(b) + _lam_init(0)


def _rms(x, gain, eps):
    return x * lax.rsqrt(jnp.mean(x * x, axis=-1, keepdims=True) + eps) * gain


def _project_kernel(x_ref, cos_ref, sin_ref, gain_ref, w_ref,
                    qs_ref, kf_ref, kb_ref, vf_ref, vb_ref, u_ref, g_ref):
    xn = _rms(x_ref[...], gain_ref[...], RMS_EPS).astype(BF16)
    z = _dot(xn, w_ref[...])
    cos = cos_ref[...]
    sin = sin_ref[...]
    lane = lax.broadcasted_iota(jnp.int32, cos.shape, 1)
    first_half = (lane % HEAD_DIM) < (HEAD_DIM // 2)
    comp0 = lane < HEAD_DIM

    def rope(xs):
        swapped = jnp.where(first_half,
                            pltpu.roll(xs, LANES - HEAD_DIM // 2, axis=1),
                            pltpu.roll(xs, HEAD_DIM // 2, axis=1))
        return xs * cos + swapped * sin

    for h in range(N_HEADS):
        sl = slice(h * V_DIM, (h + 1) * V_DIM)
        q = rope(z[:, sl]) * (HEAD_DIM ** -0.5)
        qs_ref[h, 0] = jnp.where(comp0, q, 0.0).astype(BF16)
        qs_ref[h, 1] = jnp.where(comp0, 0.0, q).astype(BF16)
        k = rope(z[:, D_ATTN + h * V_DIM:D_ATTN + (h + 1) * V_DIM])
        kf_ref[:, sl] = k
        kb_ref[h] = k.astype(BF16)
        v = z[:, 2 * D_ATTN + h * V_DIM:2 * D_ATTN + (h + 1) * V_DIM]
        vf_ref[:, sl] = v
        vb_ref[h] = v.astype(BF16)
    u_ref[...] = z[:, 3 * D_ATTN:3 * D_ATTN + D_POOL]
    g_ref[...] = z[:, 3 * D_ATTN + D_POOL:]


def _project(x, cos, sin, gain, w_in_b, block_rows):
    rows = x.shape[0]
    assert rows % block_rows == 0
    row_spec = lambda cols: pl.BlockSpec((block_rows, cols), lambda i: (i, 0))
    head_spec = pl.BlockSpec((N_HEADS, block_rows, V_DIM), lambda i: (0, i, 0))
    return pl.pallas_call(
        _project_kernel,
        grid=(rows // block_rows,),
        in_specs=[row_spec(D_MODEL), row_spec(LANES), row_spec(LANES),
                  pl.BlockSpec((1, D_MODEL), lambda i: (0, 0)),
                  pl.BlockSpec((D_MODEL, D_IN), lambda i: (0, 0))],
        out_specs=[pl.BlockSpec((N_HEADS, 2, block_rows, V_DIM), lambda i: (0, 0, i, 0)),
                   row_spec(D_ATTN), head_spec, row_spec(D_ATTN), head_spec,
                   row_spec(D_POOL), row_spec(D_MODEL)],
        out_shape=[jax.ShapeDtypeStruct((N_HEADS, 2, rows, V_DIM), BF16),
                   jax.ShapeDtypeStruct((rows, D_ATTN), F32),
                   jax.ShapeDtypeStruct((N_HEADS, rows, V_DIM), BF16),
                   jax.ShapeDtypeStruct((rows, D_ATTN), F32),
                   jax.ShapeDtypeStruct((N_HEADS, rows, V_DIM), BF16),
                   jax.ShapeDtypeStruct((rows, D_POOL), F32),
                   jax.ShapeDtypeStruct((rows, D_MODEL), F32)],
        compiler_params=pltpu.CompilerParams(
            dimension_semantics=("arbitrary",), vmem_limit_bytes=VMEM_LIMIT),
        name="project",
    )(x, cos, sin, gain, w_in_b)


def _rope_tables(pos):
    half = HEAD_DIM // 2
    inv_freq = ROPE_THETA ** (-jnp.arange(0, HEAD_DIM, 2, dtype=F32) / HEAD_DIM)
    ang = pos.astype(F32)[:, None] * inv_freq[None, :]
    cos = jnp.cos(ang)
    sin = jnp.sin(ang)
    reps = LANES // HEAD_DIM
    return (jnp.tile(cos, (1, 2 * reps)),
            jnp.tile(jnp.concatenate([-sin, sin], axis=1), (1, reps)))


def _softmax_step(s, m, l, acc, v):
    m_new = jnp.maximum(m, jnp.max(s, axis=1, keepdims=True))
    alpha = jnp.exp(m - m_new)
    p = jnp.exp(s - m_new)
    l = alpha * l + jnp.sum(p, axis=1, keepdims=True)
    acc = alpha * acc + _dot(p.astype(BF16), v)
    return m_new, l, acc


def _prompt_attn_kernel(qs_ref, k_ref, v_ref, km_ref, vm_ref,
                        lq1_ref, lk1_ref, lq2_ref, lk2_ref, sg_ref, o_ref):
    i = pl.program_id(1)
    tq, tk = ATTN_TQ, ATTN_TK
    q = qs_ref[0].reshape(2 * tq, V_DIM)

    s = _dot_nt(q, km_ref[0])
    col = lax.broadcasted_iota(jnp.int32, s.shape, 1)
    s = jnp.where(col < N_META, s, NEG_BIG)
    m = jnp.max(s, axis=1, keepdims=True)
    p = jnp.exp(s - m)
    l = jnp.sum(p, axis=1, keepdims=True)
    acc = _dot(p.astype(BF16), vm_ref[0])

    n_full = (i * tq) // tk

    def body(j, carry):
        m, l, acc = carry
        start = pl.multiple_of(j * tk, tk)
        s = _dot_nt(q, k_ref[0, pl.ds(start, tk), :])
        return _softmax_step(s, m, l, acc, v_ref[0, pl.ds(start, tk), :])

    m, l, acc = lax.fori_loop(0, n_full, body, (m, l, acc))

    start = pl.multiple_of(n_full * tk, tk)
    s = _dot_nt(q, k_ref[0, pl.ds(start, tk), :])
    row = lax.broadcasted_iota(jnp.int32, s.shape, 0)
    row = jnp.where(row >= tq, row - tq, row) + i * tq
    col = lax.broadcasted_iota(jnp.int32, s.shape, 1) + start
    s = jnp.where(col <= row, s, NEG_BIG)
    m, l, acc = _softmax_step(s, m, l, acc, v_ref[0, pl.ds(start, tk), :])

    lam = _lam(lq1_ref[...], lk1_ref[...], lq2_ref[...], lk2_ref[...])
    o = acc[:tq] / l[:tq] - lam * (acc[tq:] / l[tq:])
    o_ref[...] = _rms(o, sg_ref[...], SUBLN_EPS) * (1.0 - _lam_init(0))


def _prompt_attention(qs, kb, vb, km, vm, lams, subln_gain):
    rows = kb.shape[1]
    assert rows % ATTN_TQ == 0 and rows % ATTN_TK == 0 and ATTN_TK % ATTN_TQ == 0
    vec = lambda n: pl.BlockSpec((1, n), lambda h, i: (0, 0))
    return pl.pallas_call(
        _prompt_attn_kernel,
        grid=(N_HEADS, rows // ATTN_TQ),
        in_specs=[pl.BlockSpec((1, 2, ATTN_TQ, V_DIM), lambda h, i: (h, 0, i, 0)),
                  pl.BlockSpec((1, rows, V_DIM), lambda h, i: (h, 0, 0)),
                  pl.BlockSpec((1, rows, V_DIM), lambda h, i: (h, 0, 0)),
                  pl.BlockSpec((1, LANES, V_DIM), lambda h, i: (h, 0, 0)),
                  pl.BlockSpec((1, LANES, V_DIM), lambda h, i: (h, 0, 0)),
                  vec(HEAD_DIM), vec(HEAD_DIM), vec(HEAD_DIM), vec(HEAD_DIM), vec(V_DIM)],
        out_specs=pl.BlockSpec((ATTN_TQ, V_DIM), lambda h, i: (i, h)),
        out_shape=jax.ShapeDtypeStruct((rows, D_ATTN), F32),
        compiler_params=pltpu.CompilerParams(
            dimension_semantics=("arbitrary", "arbitrary"), vmem_limit_bytes=VMEM_LIMIT),
        name="prompt_attention",
    )(qs, kb, vb, km, vm, *lams, subln_gain)


def _decode_attn_kernel(pt_ref, q_ref, kn_ref, vn_ref, *rest):
    n = PAGES_PER_STEP
    k_refs = rest[:n]
    v_refs = rest[n:2 * n]
    lq1_ref, lk1_ref, lq2_ref, lk2_ref, sg_ref, o_ref, kbuf, vbuf, m_ref, l_ref, acc_ref = rest[2 * n:]
    step = pl.program_id(1)
    n_new = q_ref.shape[2] // 2

    def attend(k_all, v_all, mask, first):
        for h in range(N_HEADS):
            sl = slice(h * V_DIM, (h + 1) * V_DIM)
            s = _dot_nt(q_ref[0, h], k_all[:, sl])
            if mask is not None:
                s = jnp.where(mask, s, NEG_BIG)
            m_cur = jnp.max(s, axis=1, keepdims=True)
            if first:
                m_new = jnp.broadcast_to(m_cur, (2 * n_new, LANES))
                p = jnp.exp(s - m_cur)
                l_ref[h] = jnp.broadcast_to(jnp.sum(p, axis=1, keepdims=True), (2 * n_new, LANES))
                acc_ref[h] = _dot(p.astype(BF16), v_all[:, sl])
            else:
                m_prev = m_ref[h]
                m_new = jnp.maximum(m_prev, m_cur)
                alpha = jnp.exp(m_prev - m_new)
                p = jnp.exp(s - m_new[:, :1])
                l_ref[h] = alpha * l_ref[h] + jnp.sum(p, axis=1, keepdims=True)
                acc_ref[h] = alpha * acc_ref[h] + _dot(p.astype(BF16), v_all[:, sl])
            m_ref[h] = m_new

    @pl.when(step == 0)
    def _():
        shape = (2 * n_new, PAGE_SIZE)
        row = lax.broadcasted_iota(jnp.int32, shape, 0) % n_new
        col = lax.broadcasted_iota(jnp.int32, shape, 1)
        attend(kn_ref[0], vn_ref[0], col <= row, True)

    for r in range(n):
        kbuf[r * PAGE_SIZE:(r + 1) * PAGE_SIZE, :] = k_refs[r][0].astype(BF16)
        vbuf[r * PAGE_SIZE:(r + 1) * PAGE_SIZE, :] = v_refs[r][0].astype(BF16)
    attend(kbuf[...], vbuf[...], None, False)

    @pl.when(step == pl.num_programs(1) - 1)
    def _():
        lam = _lam(lq1_ref[...], lk1_ref[...], lq2_ref[...], lk2_ref[...])
        for h in range(N_HEADS):
            o = acc_ref[h] / l_ref[h]
            o = o - lam * pltpu.roll(o, n_new, axis=0)
            o = _rms(o, sg_ref[...], SUBLN_EPS) * (1.0 - _lam_init(0))
            o_ref[0, :, h * V_DIM:(h + 1) * V_DIM] = o[:n_new]


def _decode_attention(page_table, q_seq, k_new, v_new, cache_k, cache_v, lams, subln_gain):
    n_seq, n_pages = page_table.shape
    n_new2 = q_seq.shape[2]
    assert n_pages % PAGES_PER_STEP == 0
    page_spec = lambda r: pl.BlockSpec(
        (1, PAGE_SIZE, D_ATTN), lambda b, s, pt: (pt[b, s * PAGES_PER_STEP + r], 0, 0))
    seq_spec = lambda *shape: pl.BlockSpec((1,) + shape, lambda b, s, pt: (b,) + (0,) * len(shape))
    vec = lambda n: pl.BlockSpec((1, n), lambda b, s, pt: (0, 0))
    grid_spec = pltpu.PrefetchScalarGridSpec(
        num_scalar_prefetch=1,
        grid=(n_seq, n_pages // PAGES_PER_STEP),
        in_specs=[seq_spec(N_HEADS, n_new2, V_DIM), seq_spec(PAGE_SIZE, D_ATTN), seq_spec(PAGE_SIZE, D_ATTN)]
                 + [page_spec(r) for r in range(PAGES_PER_STEP)] * 2
                 + [vec(HEAD_DIM)] * 4 + [vec(V_DIM)],
        out_specs=seq_spec(n_new2 // 2, D_ATTN),
        scratch_shapes=[pltpu.VMEM((PAGES_PER_STEP * PAGE_SIZE, D_ATTN), BF16),
                        pltpu.VMEM((PAGES_PER_STEP * PAGE_SIZE, D_ATTN), BF16),
                        pltpu.VMEM((N_HEADS, n_new2, LANES), F32),
                        pltpu.VMEM((N_HEADS, n_new2, LANES), F32),
                        pltpu.VMEM((N_HEADS, n_new2, V_DIM), F32)],
    )
    return pl.pallas_call(
        _decode_attn_kernel,
        grid_spec=grid_spec,
        out_shape=jax.ShapeDtypeStruct((n_seq, n_new2 // 2, D_ATTN), F32),
        compiler_params=pltpu.CompilerParams(
            dimension_semantics=("arbitrary", "arbitrary"), vmem_limit_bytes=VMEM_LIMIT),
        name="decode_attention",
    )(page_table, q_seq, k_new, v_new, *([cache_k] * PAGES_PER_STEP), *([cache_v] * PAGES_PER_STEP),
      *lams, subln_gain)


def _merge_tail(attn, pool_diffs, gate, x, wp_ref, ps_ref, wo_ref, fg_ref):
    pool = jnp.concatenate(
        [_dot(pool_diffs[g].astype(BF16), wp_ref[g]) for g in range(len(POOL_WINDOWS))], axis=1)
    mixed = jnp.concatenate([attn, pool * ps_ref[...]], axis=1)
    mixed = mixed * (gate * jax.nn.sigmoid(gate))
    h = x + _dot(mixed.astype(BF16), wo_ref[...])
    return _rms(h, fg_ref[...], RMS_EPS)


def _prompt_merge_kernel(attn_ref, u_ref, um_ref, gate_ref, x_ref, wp_ref, ps_ref, wo_ref, fg_ref,
                         y_ref, ubuf):
    rows = u_ref.shape[0]

    @pl.when(pl.program_id(0) == 0)
    def _():
        ubuf[0:HALO, :] = um_ref[...]

    ubuf[HALO:HALO + rows, :] = u_ref[...]
    diffs = []
    for g, w in enumerate(POOL_WINDOWS):
        sl = slice(g * POOL_GROUP, (g + 1) * POOL_GROUP)
        total = ubuf[HALO:HALO + rows, sl]
        for back in range(1, w):
            total = total + ubuf[HALO - back:HALO - back + rows, sl]
        diffs.append(total / float(w) - ubuf[HALO:HALO + rows, sl])
    y_ref[...] = _merge_tail(attn_ref[...], diffs, gate_ref[...], x_ref[...],
                             wp_ref, ps_ref, wo_ref, fg_ref)
    ubuf[0:HALO, :] = ubuf[rows:rows + HALO, :]


def _prompt_merge(attn, u, u_meta, gate, x, wp_b, pool_scale, wo_b, final_gain):
    rows = x.shape[0]
    t = MERGE_ROWS
    assert rows % t == 0 and u_meta.shape[0] == HALO
    row_spec = lambda cols: pl.BlockSpec((t, cols), lambda i: (i, 0))
    full = lambda *shape: pl.BlockSpec(shape, lambda i: (0,) * len(shape))
    return pl.pallas_call(
        _prompt_merge_kernel,
        grid=(rows // t,),
        in_specs=[row_spec(D_ATTN), row_spec(D_POOL), full(HALO, D_POOL), row_spec(D_MODEL),
                  row_spec(D_MODEL), full(len(POOL_WINDOWS), POOL_GROUP, POOL_GROUP),
                  full(1, D_POOL), full(D_MODEL, D_MODEL), full(1, D_MODEL)],
        out_specs=row_spec(D_MODEL),
        out_shape=jax.ShapeDtypeStruct((rows, D_MODEL), F32),
        scratch_shapes=[pltpu.VMEM((t + HALO, D_POOL), F32)],
        compiler_params=pltpu.CompilerParams(
            dimension_semantics=("arbitrary",), vmem_limit_bytes=VMEM_LIMIT),
        name="prompt_merge",
    )(attn, u, u_meta, gate, x, wp_b, pool_scale, wo_b, final_gain)


def _sample_merge_kernel(attn_ref, ush_ref, gate_ref, x_ref, wp_ref, ps_ref, wo_ref, fg_ref, y_ref):
    diffs = []
    for g, w in enumerate(POOL_WINDOWS):
        sl = slice(g * POOL_GROUP, (g + 1) * POOL_GROUP)
        total = ush_ref[0, :, sl]
        for back in range(1, w):
            total = total + ush_ref[back, :, sl]
        diffs.append(total / float(w) - ush_ref[0, :, sl])
    y_ref[...] = _merge_tail(attn_ref[...], diffs, gate_ref[...], x_ref[...],
                             wp_ref, ps_ref, wo_ref, fg_ref)


def _sample_merge(attn, u_shifted, gate, x, wp_b, pool_scale, wo_b, final_gain):
    rows = x.shape[0]
    return pl.pallas_call(
        _sample_merge_kernel,
        out_shape=jax.ShapeDtypeStruct((rows, D_MODEL), F32),
        compiler_params=pltpu.CompilerParams(vmem_limit_bytes=VMEM_LIMIT),
        name="sample_merge",
    )(attn, u_shifted, gate, x, wp_b, pool_scale, wo_b, final_gain)


def kernel(x_prompt, x_sample, cache_k, cache_v, state_pool, page_table, meta_tokens, norm_gain, w_in,
           lambda_q1, lambda_k1, lambda_q2, lambda_k2, subln_gain, w_pool, pool_scale, w_out, final_gain):
    batch, seq, _ = x_prompt.shape
    n_seq, n_new, _ = x_sample.shape
    assert batch == 1 and cache_k.shape[0] == 1
    p_len = page_table.shape[1] * PAGE_SIZE
    assert N_META + 1 >= max(POOL_WINDOWS) and p_len - POOL_STATE + 1 >= max(POOL_WINDOWS)

    w_in_b = w_in[0].astype(BF16)
    w_out_b = w_out[0].astype(BF16)
    w_pool_b = w_pool[0].astype(BF16)
    lams = (lambda_q1, lambda_k1, lambda_q2, lambda_k2)
    final_gain2 = final_gain[None, :]

    cos_p, sin_p = _rope_tables(N_META + jnp.arange(seq))
    qs_p, kf_p, kb_p, vf_p, vb_p, u_p, g_p = _project(
        x_prompt[0], cos_p, sin_p, norm_gain, w_in_b, PROJ_ROWS)
    n_small = N_META + n_seq * n_new
    pos_small = jnp.concatenate([jnp.arange(N_META), jnp.tile(p_len + jnp.arange(n_new), n_seq)])
    cos_s, sin_s = _rope_tables(pos_small)
    x_small = jnp.concatenate([meta_tokens, x_sample.reshape(n_seq * n_new, D_MODEL)], axis=0)
    qs_s, kf_s, kb_s, vf_s, vb_s, u_s, g_s = _project(
        x_small, cos_s, sin_s, norm_gain, w_in_b, n_small)

    pad_meta = ((0, 0), (0, LANES - N_META), (0, 0))
    attn_p = _prompt_attention(qs_p, kb_p, vb_p, jnp.pad(kb_s[:, :N_META], pad_meta),
                               jnp.pad(vb_s[:, :N_META], pad_meta), lams, subln_gain)
    y_prompt = _prompt_merge(attn_p, u_p, u_s[:N_META], g_p, x_prompt[0], w_pool_b, pool_scale,
                             w_out_b, final_gain2)

    q_seq = qs_s[:, :, N_META:].reshape(N_HEADS, 2, n_seq, n_new, V_DIM)
    q_seq = q_seq.transpose(2, 0, 1, 3, 4).reshape(n_seq, N_HEADS, 2 * n_new, V_DIM)
    pad_new = ((0, 0), (0, PAGE_SIZE - n_new), (0, 0))
    k_new = jnp.pad(kf_s[N_META:].astype(BF16).reshape(n_seq, n_new, D_ATTN), pad_new)
    v_new = jnp.pad(vf_s[N_META:].astype(BF16).reshape(n_seq, n_new, D_ATTN), pad_new)
    n_pool_pages = cache_k.shape[1]
    attn_s = _decode_attention(page_table, q_seq, k_new, v_new,
                               cache_k.reshape(n_pool_pages, PAGE_SIZE, D_ATTN),
                               cache_v.reshape(n_pool_pages, PAGE_SIZE, D_ATTN), lams, subln_gain)
    u_new = u_s[N_META:].reshape(n_seq, n_new, D_POOL)
    u_ext = jnp.concatenate([state_pool[0], u_new], axis=1)
    u_shifted = jnp.stack([u_ext[:, POOL_STATE - back:POOL_STATE - back + n_new].reshape(-1, D_POOL)
                           for back in range(max(POOL_WINDOWS))])
    y_sample = _sample_merge(attn_s.reshape(n_seq * n_new, D_ATTN), u_shifted, g_s[N_META:],
                             x_sample.reshape(n_seq * n_new, D_MODEL), w_pool_b, pool_scale,
                             w_out_b, final_gain2)

    l_p = seq + N_META
    new_k_prompt = jnp.concatenate([kf_s[:N_META], kf_p]).reshape(1, 1, l_p, N_HEADS, V_DIM)
    new_v_prompt = jnp.concatenate([vf_s[:N_META], vf_p]).reshape(1, 1, l_p, N_HEADS, V_DIM)
    new_pool_prompt = u_p[-POOL_STATE:].reshape(1, 1, POOL_STATE, D_POOL)
    new_k_sample = kf_s[N_META:].reshape(1, n_seq, n_new, N_HEADS, V_DIM)
    new_v_sample = vf_s[N_META:].reshape(1, n_seq, n_new, N_HEADS, V_DIM)
    new_pool_sample = u_ext[:, -POOL_STATE:][None]
    return (y_prompt[None], y_sample.reshape(n_seq, n_new, D_MODEL), new_k_prompt, new_v_prompt,
            new_pool_prompt, new_k_sample, new_v_sample, new_pool_sample)
```

```python
import functools
import math

import jax
import jax.numpy as jnp
from jax import lax
from jax.experimental import pallas as pl
from jax.experimental.pallas import tpu as pltpu

F32 = jnp.float32
BF16 = jnp.bfloat16

D_MODEL = 1024
N_META = 16
D_ATTN = 512
D_POOL = 512
HEAD_DIM = 64
N_HEADS = 4
V_DIM = 128
POOL_WINDOWS = (2, 4, 8, 16)
POOL_GROUP = 128
POOL_STATE = 15
D_IN = 3 * D_ATTN + D_POOL + D_MODEL
ROPE_THETA = 10000.0
RMS_EPS = 1e-6
SUBLN_EPS = 1e-5
PAGE_SIZE = 128
LANES = 128
HALO = 16
NEG_BIG = -1e30

PROJ_ROWS = 512
ATTN_TQ = 256
ATTN_TK = 512
MERGE_ROWS = 512
PAGES_PER_STEP = 8
VMEM_LIMIT = 48 * 1024 * 1024


def _dot(a, b):
    return jnp.dot(a, b, preferred_element_type=F32)


def _dot_nt(a, b):
    return lax.dot_general(a, b, (((1,), (1,)), ((), ())), preferred_element_type=F32)


def _lam_init(layer):
    return 0.8 - 0.6 * math.exp(-0.3 * layer)


def _lam(lq1, lk1, lq2, lk2):
    a = jnp.sum(lq1 * lk1, axis=1, keepdims=True)
    b = jnp.sum(lq2 * lk2, axis=1, keepdims=True)
    return jnp.exp(a) - jnp.exp(b) + _lam_init(0)


def _rms(x, gain, eps):
    return x * lax.rsqrt(jnp.mean(x * x, axis=-1, keepdims=True) + eps) * gain


def _project_kernel(x_ref, cos_ref, sin_ref, gain_ref, w_ref,
                    qs_ref, kf_ref, kb_ref, vf_ref, vb_ref, u_ref, g_ref):
    rows = x_ref.shape[0]
    xn = _rms(x_ref[...], gain_ref[...], RMS_EPS).astype(BF16)
    z = _dot(xn, w_ref[...])
    cos = cos_ref[...]
    sin = sin_ref[...]
    lane = lax.broadcasted_iota(jnp.int32, cos.shape, 1)
    first_half = (lane % HEAD_DIM) < (HEAD_DIM // 2)
    comp0 = lane < HEAD_DIM

    def rope(xs):
        swapped = jnp.where(first_half,
                            pltpu.roll(xs, LANES - HEAD_DIM // 2, axis=1),
                            pltpu.roll(xs, HEAD_DIM // 2, axis=1))
        return xs * cos + swapped * sin

    for h in range(N_HEADS):
        sl = slice(h * V_DIM, (h + 1) * V_DIM)
        q = rope(z[:, sl]) * (HEAD_DIM ** -0.5)
        qs_ref[h, 0] = jnp.where(comp0, q, 0.0).astype(BF16)
        qs_ref[h, 1] = jnp.where(comp0, 0.0, q).astype(BF16)
        k = rope(z[:, D_ATTN + h * V_DIM:D_ATTN + (h + 1) * V_DIM])
        kf_ref[pl.ds(h, rows, stride=N_HEADS), :] = k
        kb_ref[h] = k.astype(BF16)
        v = z[:, 2 * D_ATTN + h * V_DIM:2 * D_ATTN + (h + 1) * V_DIM]
        vf_ref[pl.ds(h, rows, stride=N_HEADS), :] = v
        vb_ref[h] = v.astype(BF16)
    u_ref[...] = z[:, 3 * D_ATTN:3 * D_ATTN + D_POOL]
    g_ref[...] = z[:, 3 * D_ATTN + D_POOL:]


def _project(x, cos, sin, gain, w_in_b, block_rows):
    rows = x.shape[0]
    assert rows % block_rows == 0
    row_spec = lambda cols: pl.BlockSpec((block_rows, cols), lambda i: (i, 0))
    head_spec = pl.BlockSpec((N_HEADS, block_rows, V_DIM), lambda i: (0, i, 0))
    cache_spec = pl.BlockSpec((block_rows * N_HEADS, V_DIM), lambda i: (i, 0))
    return pl.pallas_call(
        _project_kernel,
        grid=(rows // block_rows,),
        in_specs=[row_spec(D_MODEL), row_spec(LANES), row_spec(LANES),
                  pl.BlockSpec((1, D_MODEL), lambda i: (0, 0)),
                  pl.BlockSpec((D_MODEL, D_IN), lambda i: (0, 0))],
        out_specs=[pl.BlockSpec((N_HEADS, 2, block_rows, V_DIM), lambda i: (0, 0, i, 0)),
                   cache_spec, head_spec, cache_spec, head_spec,
                   row_spec(D_POOL), row_spec(D_MODEL)],
        out_shape=[jax.ShapeDtypeStruct((N_HEADS, 2, rows, V_DIM), BF16),
                   jax.ShapeDtypeStruct((rows * N_HEADS, V_DIM), F32),
                   jax.ShapeDtypeStruct((N_HEADS, rows, V_DIM), BF16),
                   jax.ShapeDtypeStruct((rows * N_HEADS, V_DIM), F32),
                   jax.ShapeDtypeStruct((N_HEADS, rows, V_DIM), BF16),
                   jax.ShapeDtypeStruct((rows, D_POOL), F32),
                   jax.ShapeDtypeStruct((rows, D_MODEL), F32)],
        compiler_params=pltpu.CompilerParams(
            dimension_semantics=("arbitrary",), vmem_limit_bytes=VMEM_LIMIT),
        name="project",
    )(x, cos, sin, gain, w_in_b)


def _rope_tables(pos):
    half = HEAD_DIM // 2
    inv_freq = ROPE_THETA ** (-jnp.arange(0, HEAD_DIM, 2, dtype=F32) / HEAD_DIM)
    ang = pos.astype(F32)[:, None] * inv_freq[None, :]
    cos = jnp.cos(ang)
    sin = jnp.sin(ang)
    reps = LANES // HEAD_DIM
    return (jnp.tile(cos, (1, 2 * reps)),
            jnp.tile(jnp.concatenate([-sin, sin], axis=1), (1, reps)))


def _softmax_step(s, m, l, acc, v):
    m_new = jnp.maximum(m, jnp.max(s, axis=1, keepdims=True))
    alpha = jnp.exp(m - m_new)
    p = jnp.exp(s - m_new)
    l = alpha * l + jnp.sum(p, axis=1, keepdims=True)
    acc = alpha * acc + _dot(p.astype(BF16), v)
    return m_new, l, acc


def _prompt_attn_kernel(qs_ref, k_ref, v_ref, km_ref, vm_ref,
                        lq1_ref, lk1_ref, lq2_ref, lk2_ref, sg_ref, o_ref):
    i = pl.program_id(1)
    tq, tk = ATTN_TQ, ATTN_TK
    q = qs_ref[0].reshape(2 * tq, V_DIM)

    s = _dot_nt(q, km_ref[0])
    col = lax.broadcasted_iota(jnp.int32, s.shape, 1)
    s = jnp.where(col < N_META, s, NEG_BIG)
    m = jnp.max(s, axis=1, keepdims=True)
    p = jnp.exp(s - m)
    l = jnp.sum(p, axis=1, keepdims=True)
    acc = _dot(p.astype(BF16), vm_ref[0])

    n_full = (i * tq) // tk

    def body(j, carry):
        m, l, acc = carry
        start = pl.multiple_of(j * tk, tk)
        s = _dot_nt(q, k_ref[0, pl.ds(start, tk), :])
        return _softmax_step(s, m, l, acc, v_ref[0, pl.ds(start, tk), :])

    m, l, acc = lax.fori_loop(0, n_full, body, (m, l, acc))

    start = pl.multiple_of(n_full * tk, tk)
    s = _dot_nt(q, k_ref[0, pl.ds(start, tk), :])
    row = lax.broadcasted_iota(jnp.int32, s.shape, 0)
    row = jnp.where(row >= tq, row - tq, row) + i * tq
    col = lax.broadcasted_iota(jnp.int32, s.shape, 1) + start
    s = jnp.where(col <= row, s, NEG_BIG)
    m, l, acc = _softmax_step(s, m, l, acc, v_ref[0, pl.ds(start, tk), :])

    lam = _lam(lq1_ref[...], lk1_ref[...], lq2_ref[...], lk2_ref[...])
    o = acc[:tq] / l[:tq] - lam * (acc[tq:] / l[tq:])
    o_ref[...] = _rms(o, sg_ref[...], SUBLN_EPS) * (1.0 - _lam_init(0))


def _prompt_attention(qs, kb, vb, km, vm, lams, subln_gain):
    rows = kb.shape[1]
    assert rows % ATTN_TQ == 0 and rows % ATTN_TK == 0 and ATTN_TK % ATTN_TQ == 0
    vec = lambda n: pl.BlockSpec((1, n), lambda h, i: (0, 0))
    return pl.pallas_call(
        _prompt_attn_kernel,
        grid=(N_HEADS, rows // ATTN_TQ),
        in_specs=[pl.BlockSpec((1, 2, ATTN_TQ, V_DIM), lambda h, i: (h, 0, i, 0)),
                  pl.BlockSpec((1, rows, V_DIM), lambda h, i: (h, 0, 0)),
                  pl.BlockSpec((1, rows, V_DIM), lambda h, i: (h, 0, 0)),
                  pl.BlockSpec((1, LANES, V_DIM), lambda h, i: (h, 0, 0)),
                  pl.BlockSpec((1, LANES, V_DIM), lambda h, i: (h, 0, 0)),
                  vec(HEAD_DIM), vec(HEAD_DIM), vec(HEAD_DIM), vec(HEAD_DIM), vec(V_DIM)],
        out_specs=pl.BlockSpec((ATTN_TQ, V_DIM), lambda h, i: (i, h)),
        out_shape=jax.ShapeDtypeStruct((rows, D_ATTN), F32),
        compiler_params=pltpu.CompilerParams(
            dimension_semantics=("arbitrary", "arbitrary"), vmem_limit_bytes=VMEM_LIMIT),
        name="prompt_attention",
    )(qs, kb, vb, km, vm, *lams, subln_gain)


def _decode_attn_kernel(pt_ref, q_ref, kn_ref, vn_ref, *rest):
    n = PAGES_PER_STEP
    k_refs = rest[:n]
    v_refs = rest[n:2 * n]
    lq1_ref, lk1_ref, lq2_ref, lk2_ref, sg_ref, o_ref, m_ref, l_ref, acc_ref = rest[2 * n:]
    step = pl.program_id(1)
    q = q_ref[0]
    n_rows = q.shape[0]
    n_new = n_rows // (2 * N_HEADS)

    def own_head(n_keys):
        row = lax.broadcasted_iota(jnp.int32, (n_rows, n_keys), 0)
        col = lax.broadcasted_iota(jnp.int32, (n_rows, n_keys), 1)
        return row, col, (col % N_HEADS) == (row // (2 * n_new))

    @pl.when(step == 0)
    def _():
        row, col, own = own_head(kn_ref.shape[1])
        s = jnp.where(own & ((col // N_HEADS) <= (row % n_new)), _dot_nt(q, kn_ref[0]), NEG_BIG)
        m = jnp.max(s, axis=1, keepdims=True)
        p = jnp.exp(s - m)
        m_ref[...] = jnp.broadcast_to(m, m_ref.shape)
        l_ref[...] = jnp.broadcast_to(jnp.sum(p, axis=1, keepdims=True), l_ref.shape)
        acc_ref[...] = _dot(p.astype(BF16), vn_ref[0])

    _, _, own = own_head(PAGE_SIZE * N_HEADS)
    scores = [jnp.where(own, _dot_nt(q, k_refs[r][0].astype(BF16)), NEG_BIG) for r in range(n)]
    m_cur = scores[0]
    for s in scores[1:]:
        m_cur = jnp.maximum(m_cur, s)
    m_prev = m_ref[...]
    m_new = jnp.maximum(m_prev, jnp.max(m_cur, axis=1, keepdims=True))
    alpha = jnp.exp(m_prev - m_new)
    l = alpha * l_ref[...]
    acc = alpha * acc_ref[...]
    for r in range(n):
        p = jnp.exp(scores[r] - m_new[:, :1])
        l = l + jnp.sum(p, axis=1, keepdims=True)
        acc = acc + _dot(p.astype(BF16), v_refs[r][0].astype(BF16))
    m_ref[...] = m_new
    l_ref[...] = l
    acc_ref[...] = acc

    @pl.when(step == pl.num_programs(1) - 1)
    def _():
        lam = _lam(lq1_ref[...], lk1_ref[...], lq2_ref[...], lk2_ref[...])
        o = acc_ref[...] / l_ref[...]
        o = o - lam * pltpu.roll(o, n_rows - n_new, axis=0)
        o = _rms(o, sg_ref[...], SUBLN_EPS) * (1.0 - _lam_init(0))
        for h in range(N_HEADS):
            o_ref[0, :, h * V_DIM:(h + 1) * V_DIM] = o[h * 2 * n_new:h * 2 * n_new + n_new]


def _decode_attention(page_table, q_seq, k_new, v_new, cache_k, cache_v, lams, subln_gain):
    n_seq, n_pages = page_table.shape
    n_rows = q_seq.shape[1]
    n_new = n_rows // (2 * N_HEADS)
    page_rows = PAGE_SIZE * N_HEADS
    assert n_pages % PAGES_PER_STEP == 0 and cache_k.shape[1:] == (page_rows, V_DIM)
    page_spec = lambda r: pl.BlockSpec(
        (1, page_rows, V_DIM), lambda b, s, pt: (pt[b, s * PAGES_PER_STEP + r], 0, 0))
    seq_spec = lambda *shape: pl.BlockSpec((1,) + shape, lambda b, s, pt: (b,) + (0,) * len(shape))
    vec = lambda n: pl.BlockSpec((1, n), lambda b, s, pt: (0, 0))
    grid_spec = pltpu.PrefetchScalarGridSpec(
        num_scalar_prefetch=1,
        grid=(n_seq, n_pages // PAGES_PER_STEP),
        in_specs=[seq_spec(n_rows, V_DIM), seq_spec(*k_new.shape[1:]), seq_spec(*v_new.shape[1:])]
                 + [page_spec(r) for r in range(PAGES_PER_STEP)] * 2
                 + [vec(HEAD_DIM)] * 4 + [vec(V_DIM)],
        out_specs=seq_spec(n_new, D_ATTN),
        scratch_shapes=[pltpu.VMEM((n_rows, LANES), F32),
                        pltpu.VMEM((n_rows, LANES), F32),
                        pltpu.VMEM((n_rows, V_DIM), F32)],
    )
    return pl.pallas_call(
        _decode_attn_kernel,
        grid_spec=grid_spec,
        out_shape=jax.ShapeDtypeStruct((n_seq, n_new, D_ATTN), F32),
        compiler_params=pltpu.CompilerParams(
            dimension_semantics=("arbitrary", "arbitrary"), vmem_limit_bytes=VMEM_LIMIT),
        name="decode_attention",
    )(page_table, q_seq, k_new, v_new, *([cache_k] * PAGES_PER_STEP), *([cache_v] * PAGES_PER_STEP),
      *lams, subln_gain)


def _merge_tail(attn, pool_diffs, gate, x, wp_ref, ps_ref, wo_ref, fg_ref):
    pool = jnp.concatenate(
        [_dot(pool_diffs[g].astype(BF16), wp_ref[g]) for g in range(len(POOL_WINDOWS))], axis=1)
    mixed = jnp.concatenate([attn, pool * ps_ref[...]], axis=1)
    mixed = mixed * (gate * jax.nn.sigmoid(gate))
    h = x + _dot(mixed.astype(BF16), wo_ref[...])
    return _rms(h, fg_ref[...], RMS_EPS)


def _prompt_merge_kernel(attn_ref, u_ref, um_ref, gate_ref, x_ref, wp_ref, ps_ref, wo_ref, fg_ref,
                         y_ref, ubuf):
    rows = u_ref.shape[0]

    @pl.when(pl.program_id(0) == 0)
    def _():
        ubuf[0:HALO, :] = um_ref[...]

    ubuf[HALO:HALO + rows, :] = u_ref[...]
    diffs = []
    for g, w in enumerate(POOL_WINDOWS):
        sl = slice(g * POOL_GROUP, (g + 1) * POOL_GROUP)
        total = ubuf[HALO:HALO + rows, sl]
        for back in range(1, w):
            total = total + ubuf[HALO - back:HALO - back + rows, sl]
        diffs.append(total / float(w) - ubuf[HALO:HALO + rows, sl])
    y_ref[...] = _merge_tail(attn_ref[...], diffs, gate_ref[...], x_ref[...],
                             wp_ref, ps_ref, wo_ref, fg_ref)
    ubuf[0:HALO, :] = ubuf[rows:rows + HALO, :]


def _prompt_merge(attn, u, u_meta, gate, x, wp_b, pool_scale, wo_b, final_gain):
    rows = x.shape[0]
    t = MERGE_ROWS
    assert rows % t == 0 and u_meta.shape[0] == HALO
    row_spec = lambda cols: pl.BlockSpec((t, cols), lambda i: (i, 0))
    full = lambda *shape: pl.BlockSpec(shape, lambda i: (0,) * len(shape))
    return pl.pallas_call(
        _prompt_merge_kernel,
        grid=(rows // t,),
        in_specs=[row_spec(D_ATTN), row_spec(D_POOL), full(HALO, D_POOL), row_spec(D_MODEL),
                  row_spec(D_MODEL), full(len(POOL_WINDOWS), POOL_GROUP, POOL_GROUP),
                  full(1, D_POOL), full(D_MODEL, D_MODEL), full(1, D_MODEL)],
        out_specs=row_spec(D_MODEL),
        out_shape=jax.ShapeDtypeStruct((rows, D_MODEL), F32),
        scratch_shapes=[pltpu.VMEM((t + HALO, D_POOL), F32)],
        compiler_params=pltpu.CompilerParams(
            dimension_semantics=("arbitrary",), vmem_limit_bytes=VMEM_LIMIT),
        name="prompt_merge",
    )(attn, u, u_meta, gate, x, wp_b, pool_scale, wo_b, final_gain)


def _sample_merge_kernel(attn_ref, ush_ref, gate_ref, x_ref, wp_ref, ps_ref, wo_ref, fg_ref, y_ref):
    diffs = []
    for g, w in enumerate(POOL_WINDOWS):
        sl = slice(g * POOL_GROUP, (g + 1) * POOL_GROUP)
        total = ush_ref[0, :, sl]
        for back in range(1, w):
            total = total + ush_ref[back, :, sl]
        diffs.append(total / float(w) - ush_ref[0, :, sl])
    y_ref[...] = _merge_tail(attn_ref[...], diffs, gate_ref[...], x_ref[...],
                             wp_ref, ps_ref, wo_ref, fg_ref)


def _sample_merge(attn, u_shifted, gate, x, wp_b, pool_scale, wo_b, final_gain):
    rows = x.shape[0]
    return pl.pallas_call(
        _sample_merge_kernel,
        out_shape=jax.ShapeDtypeStruct((rows, D_MODEL), F32),
        compiler_params=pltpu.CompilerParams(vmem_limit_bytes=VMEM_LIMIT),
        name="sample_merge",
    )(attn, u_shifted, gate, x, wp_b, pool_scale, wo_b, final_gain)


def kernel(x_prompt, x_sample, cache_k, cache_v, state_pool, page_table, meta_tokens, norm_gain, w_in,
           lambda_q1, lambda_k1, lambda_q2, lambda_k2, subln_gain, w_pool, pool_scale, w_out, final_gain):
    batch, seq, _ = x_prompt.shape
    n_seq, n_new, _ = x_sample.shape
    assert batch == 1 and cache_k.shape[0] == 1
    p_len = page_table.shape[1] * PAGE_SIZE
    assert N_META + 1 >= max(POOL_WINDOWS) and p_len - POOL_STATE + 1 >= max(POOL_WINDOWS)

    w_in_b = w_in[0].astype(BF16)
    w_out_b = w_out[0].astype(BF16)
    w_pool_b = w_pool[0].astype(BF16)
    lams = (lambda_q1, lambda_k1, lambda_q2, lambda_k2)
    final_gain2 = final_gain[None, :]

    cos_p, sin_p = _rope_tables(N_META + jnp.arange(seq))
    qs_p, kf_p, kb_p, vf_p, vb_p, u_p, g_p = _project(
        x_prompt[0], cos_p, sin_p, norm_gain, w_in_b, PROJ_ROWS)
    n_small = N_META + n_seq * n_new
    pos_small = jnp.concatenate([jnp.arange(N_META), jnp.tile(p_len + jnp.arange(n_new), n_seq)])
    cos_s, sin_s = _rope_tables(pos_small)
    x_small = jnp.concatenate([meta_tokens, x_sample.reshape(n_seq * n_new, D_MODEL)], axis=0)
    qs_s, kf_s, kb_s, vf_s, vb_s, u_s, g_s = _project(
        x_small, cos_s, sin_s, norm_gain, w_in_b, n_small)

    pad_meta = ((0, 0), (0, LANES - N_META), (0, 0))
    attn_p = _prompt_attention(qs_p, kb_p, vb_p, jnp.pad(kb_s[:, :N_META], pad_meta),
                               jnp.pad(vb_s[:, :N_META], pad_meta), lams, subln_gain)
    y_prompt = _prompt_merge(attn_p, u_p, u_s[:N_META], g_p, x_prompt[0], w_pool_b, pool_scale,
                             w_out_b, final_gain2)

    q_seq = qs_s[:, :, N_META:].reshape(N_HEADS, 2, n_seq, n_new, V_DIM)
    q_seq = q_seq.transpose(2, 0, 1, 3, 4).reshape(n_seq, N_HEADS * 2 * n_new, V_DIM)
    pad_new = ((0, 0), (0, LANES - n_new * N_HEADS), (0, 0))
    k_new = jnp.pad(kf_s[N_META * N_HEADS:].astype(BF16).reshape(n_seq, n_new * N_HEADS, V_DIM), pad_new)
    v_new = jnp.pad(vf_s[N_META * N_HEADS:].astype(BF16).reshape(n_seq, n_new * N_HEADS, V_DIM), pad_new)
    n_pool_pages = cache_k.shape[1]
    attn_s = _decode_attention(page_table, q_seq, k_new, v_new,
                               cache_k.reshape(n_pool_pages, PAGE_SIZE * N_HEADS, V_DIM),
                               cache_v.reshape(n_pool_pages, PAGE_SIZE * N_HEADS, V_DIM), lams, subln_gain)
    u_new = u_s[N_META:].reshape(n_seq, n_new, D_POOL)
    u_ext = jnp.concatenate([state_pool[0], u_new], axis=1)
    u_shifted = jnp.stack([u_ext[:, POOL_STATE - back:POOL_STATE - back + n_new].reshape(-1, D_POOL)
                           for back in range(max(POOL_WINDOWS))])
    y_sample = _sample_merge(attn_s.reshape(n_seq * n_new, D_ATTN), u_shifted, g_s[N_META:],
                             x_sample.reshape(n_seq * n_new, D_MODEL), w_pool_b, pool_scale,
                             w_out_b, final_gain2)

    l_p = seq + N_META
    n_meta_rows = N_META * N_HEADS
    new_k_prompt = jnp.concatenate([kf_s[:n_meta_rows], kf_p]).reshape(1, 1, l_p, N_HEADS, V_DIM)
    new_v_prompt = jnp.concatenate([vf_s[:n_meta_rows], vf_p]).reshape(1, 1, l_p, N_HEADS, V_DIM)
    new_pool_prompt = u_p[-POOL_STATE:].reshape(1, 1, POOL_STATE, D_POOL)
    new_k_sample = kf_s[n_meta_rows:].reshape(1, n_seq, n_new, N_HEADS, V_DIM)
    new_v_sample = vf_s[n_meta_rows:].reshape(1, n_seq, n_new, N_HEADS, V_DIM)
    new_pool_sample = u_ext[:, -POOL_STATE:][None]
    return (y_prompt[None], y_sample.reshape(n_seq, n_new, D_MODEL), new_k_prompt, new_v_prompt,
            new_pool_prompt, new_k_sample, new_v_sample, new_pool_sample)
```

```python
import math

import jax
import jax.numpy as jnp
from jax import lax
from jax.experimental import pallas as pl
from jax.experimental.pallas import tpu as pltpu

F32 = jnp.float32
BF16 = jnp.bfloat16

D_MODEL = 1024
N_META = 16
D_ATTN = 512
D_POOL = 512
HEAD_DIM = 64
N_HEADS = 4
V_DIM = 128
POOL_WINDOWS = (2, 4, 8, 16)
POOL_GROUP = 128
POOL_STATE = 15
D_IN = 3 * D_ATTN + D_POOL + D_MODEL
ROPE_THETA = 10000.0
RMS_EPS = 1e-6
SUBLN_EPS = 1e-5
PAGE_SIZE = 128
LANES = 128
HALO = 16
NEG_BIG = -1e30
Q_SCALE = HEAD_DIM ** -0.5 * math.log2(math.e)

PROJ_ROWS = 256
SMALL_ROWS = 256
ATTN_TQ = 256
ATTN_TK = 512
MERGE_ROWS = 512
PAGES_PER_STEP = 8
VMEM_LIMIT = 48 * 1024 * 1024


def _dot(a, b):
    return jnp.dot(a, b, preferred_element_type=F32)


def _dot_nt(a, b):
    return lax.dot_general(a, b, (((1,), (1,)), ((), ())), preferred_element_type=F32)


def _lam_init(layer):
    return 0.8 - 0.6 * math.exp(-0.3 * layer)


def _lam(lq1, lk1, lq2, lk2):
    a = jnp.sum(lq1 * lk1, axis=1, keepdims=True)
    b = jnp.sum(lq2 * lk2, axis=1, keepdims=True)
    return jnp.exp(a) - jnp.exp(b) + _lam_init(0)


def _rms(x, gain, eps):
    return x * lax.rsqrt(jnp.mean(x * x, axis=-1, keepdims=True) + eps) * gain


def _project_kernel(x_ref, cos_ref, sin_ref, cost_ref, sint_ref, gain_ref, wn_ref, wt_ref,
                    qt_ref, kf_ref, kb_ref, vf_ref, vt_ref, u_ref, g_ref):
    rows = x_ref.shape[0]
    half = HEAD_DIM // 2
    xn = _rms(x_ref[...], gain_ref[...], RMS_EPS).astype(BF16)
    z = _dot(xn, wn_ref[...])
    zt = _dot_nt(wt_ref[...], xn)
    cos = cos_ref[...]
    sin = sin_ref[...]
    cost = cost_ref[...]
    sint = sint_ref[...]
    lane = lax.broadcasted_iota(jnp.int32, cos.shape, 1)
    first_half = (lane % HEAD_DIM) < half
    zeros = jnp.zeros((HEAD_DIM, rows), F32)

    for h in range(N_HEADS):
        for c in range(2):
            base = h * V_DIM + c * HEAD_DIM
            x1 = zt[base:base + half]
            x2 = zt[base + half:base + HEAD_DIM]
            o1 = (x1 * cost - x2 * sint) * Q_SCALE
            o2 = (x2 * cost + x1 * sint) * Q_SCALE
            parts = [o1, o2, zeros] if c == 0 else [zeros, o1, o2]
            qt_ref[h, c] = jnp.concatenate(parts, axis=0).astype(BF16)
        vt_ref[h] = zt[D_ATTN + h * V_DIM:D_ATTN + (h + 1) * V_DIM].astype(BF16)

        ks = z[:, h * V_DIM:(h + 1) * V_DIM]
        swapped = jnp.where(first_half, pltpu.roll(ks, LANES - half, axis=1), pltpu.roll(ks, half, axis=1))
        k = ks * cos + swapped * sin
        kf_ref[pl.ds(h, rows, stride=N_HEADS), :] = k
        kb_ref[h] = k.astype(BF16)
        vf_ref[pl.ds(h, rows, stride=N_HEADS), :] = z[:, D_ATTN + h * V_DIM:D_ATTN + (h + 1) * V_DIM]
    u_ref[...] = z[:, 2 * D_ATTN:2 * D_ATTN + D_POOL]
    g_ref[...] = z[:, 2 * D_ATTN + D_POOL:]


def _project(x, tables, gain, w_n, w_t, block_rows):
    rows = x.shape[0]
    assert rows % block_rows == 0
    cos, sin, cost, sint = tables
    row_spec = lambda cols: pl.BlockSpec((block_rows, cols), lambda i: (i, 0))
    col_spec = pl.BlockSpec((HEAD_DIM // 2, block_rows), lambda i: (0, i))
    full = lambda *shape: pl.BlockSpec(shape, lambda i: (0,) * len(shape))
    cache_spec = pl.BlockSpec((block_rows * N_HEADS, V_DIM), lambda i: (i, 0))
    return pl.pallas_call(
        _project_kernel,
        grid=(rows // block_rows,),
        in_specs=[row_spec(D_MODEL), row_spec(LANES), row_spec(LANES), col_spec, col_spec,
                  full(1, D_MODEL), full(*w_n.shape), full(*w_t.shape)],
        out_specs=[pl.BlockSpec((N_HEADS, 2, V_DIM, block_rows), lambda i: (0, 0, 0, i)),
                   cache_spec,
                   pl.BlockSpec((N_HEADS, block_rows, V_DIM), lambda i: (0, i, 0)),
                   cache_spec,
                   pl.BlockSpec((N_HEADS, V_DIM, block_rows), lambda i: (0, 0, i)),
                   row_spec(D_POOL), row_spec(D_MODEL)],
        out_shape=[jax.ShapeDtypeStruct((N_HEADS, 2, V_DIM, rows), BF16),
                   jax.ShapeDtypeStruct((rows * N_HEADS, V_DIM), F32),
                   jax.ShapeDtypeStruct((N_HEADS, rows, V_DIM), BF16),
                   jax.ShapeDtypeStruct((rows * N_HEADS, V_DIM), F32),
                   jax.ShapeDtypeStruct((N_HEADS, V_DIM, rows), BF16),
                   jax.ShapeDtypeStruct((rows, D_POOL), F32),
                   jax.ShapeDtypeStruct((rows, D_MODEL), F32)],
        compiler_params=pltpu.CompilerParams(
            dimension_semantics=("arbitrary",), vmem_limit_bytes=VMEM_LIMIT),
        name="project",
    )(x, cos, sin, cost, sint, gain, w_n, w_t)


def _rope_tables(pos):
    inv_freq = ROPE_THETA ** (-jnp.arange(0, HEAD_DIM, 2, dtype=F32) / HEAD_DIM)
    ang = pos.astype(F32)[:, None] * inv_freq[None, :]
    cos = jnp.cos(ang)
    sin = jnp.sin(ang)
    reps = LANES // HEAD_DIM
    return (jnp.tile(cos, (1, 2 * reps)), jnp.tile(jnp.concatenate([-sin, sin], axis=1), (1, reps)),
            cos.T, sin.T)


def _prompt_attn_kernel(qt_ref, k_ref, vt_ref, km_ref, vtm_ref,
                        lq1_ref, lk1_ref, lq2_ref, lk2_ref, sg_ref, o_ref,
                        m_ref, l_ref, acc_ref, sa_ref, sb_ref):
    i = pl.program_id(1)
    tq, tk = ATTN_TQ, ATTN_TK
    qt = jnp.concatenate([qt_ref[0, 0], qt_ref[0, 1]], axis=1)

    def scores(start, size):
        return _dot(k_ref[0, pl.ds(start, size), :], qt)

    def update(s, vt):
        m_prev = m_ref[...]
        m_new = jnp.maximum(m_prev, jnp.max(s, axis=0, keepdims=True))
        alpha = jnp.exp2(m_prev - m_new)
        p = jnp.exp2(s - m_new)
        l_ref[...] = alpha * l_ref[...] + jnp.sum(p, axis=0, keepdims=True)
        acc_ref[...] = alpha * acc_ref[...] + _dot(vt, p.astype(BF16))
        m_ref[...] = m_new

    m_ref[...] = jnp.full(m_ref.shape, NEG_BIG, F32)
    l_ref[...] = jnp.zeros(l_ref.shape, F32)
    acc_ref[...] = jnp.zeros(acc_ref.shape, F32)

    key = lax.broadcasted_iota(jnp.int32, (LANES, 2 * tq), 0)
    update(jnp.where(key < N_META, _dot(km_ref[0], qt), NEG_BIG), vtm_ref[0])

    n_pairs = (i * tq) // (2 * tk)

    @pl.when(n_pairs > 0)
    def _():
        sa_ref[...] = scores(0, tk)

    def body(t, carry):
        first = pl.multiple_of(2 * t * tk, 2 * tk)
        sb_ref[...] = scores(first + tk, tk)
        update(sa_ref[...], vt_ref[0, :, pl.ds(first, tk)])
        nxt = pl.multiple_of(jnp.minimum(first + 2 * tk, (2 * n_pairs - 1) * tk), tk)
        sa_ref[...] = scores(nxt, tk)
        update(sb_ref[...], vt_ref[0, :, pl.ds(first + tk, tk)])
        return carry

    lax.fori_loop(0, n_pairs, body, 0)

    def tail(j, carry):
        start = pl.multiple_of(j * tq, tq)
        update(scores(start, tq), vt_ref[0, :, pl.ds(start, tq)])
        return carry

    lax.fori_loop(n_pairs * (2 * tk // tq), i, tail, 0)

    start = pl.multiple_of(i * tq, tq)
    key = lax.broadcasted_iota(jnp.int32, (tq, 2 * tq), 0)
    qry = lax.broadcasted_iota(jnp.int32, (tq, 2 * tq), 1) % tq
    update(jnp.where(key <= qry, scores(start, tq), NEG_BIG), vt_ref[0, :, pl.ds(start, tq)])

    lam = _lam(lq1_ref[...], lk1_ref[...], lq2_ref[...], lk2_ref[...])
    o = acc_ref[...] / l_ref[...]
    o = o[:, :tq] - lam * o[:, tq:]
    o = o * lax.rsqrt(jnp.mean(o * o, axis=0, keepdims=True) + SUBLN_EPS) * sg_ref[...]
    o_ref[...] = (o * (1.0 - _lam_init(0))).T


def _prompt_attention(qt, kb, vt, km, vtm, lams, subln_gain_col):
    rows = kb.shape[1]
    assert rows % ATTN_TK == 0 and ATTN_TK % ATTN_TQ == 0
    vec = lambda n: pl.BlockSpec((1, n), lambda h, i: (0, 0))
    head = lambda *shape: pl.BlockSpec((1,) + shape, lambda h, i: (h,) + (0,) * len(shape))
    return pl.pallas_call(
        _prompt_attn_kernel,
        grid=(N_HEADS, rows // ATTN_TQ),
        in_specs=[pl.BlockSpec((1, 2, V_DIM, ATTN_TQ), lambda h, i: (h, 0, 0, i)),
                  head(rows, V_DIM), head(V_DIM, rows), head(LANES, V_DIM), head(V_DIM, LANES),
                  vec(HEAD_DIM), vec(HEAD_DIM), vec(HEAD_DIM), vec(HEAD_DIM),
                  pl.BlockSpec((V_DIM, 1), lambda h, i: (0, 0))],
        out_specs=pl.BlockSpec((ATTN_TQ, V_DIM), lambda h, i: (i, h)),
        out_shape=jax.ShapeDtypeStruct((rows, D_ATTN), F32),
        scratch_shapes=[pltpu.VMEM((1, 2 * ATTN_TQ), F32),
                        pltpu.VMEM((1, 2 * ATTN_TQ), F32),
                        pltpu.VMEM((V_DIM, 2 * ATTN_TQ), F32),
                        pltpu.VMEM((ATTN_TK, 2 * ATTN_TQ), F32),
                        pltpu.VMEM((ATTN_TK, 2 * ATTN_TQ), F32)],
        compiler_params=pltpu.CompilerParams(
            dimension_semantics=("arbitrary", "arbitrary"), vmem_limit_bytes=VMEM_LIMIT),
        name="prompt_attention",
    )(qt, kb, vt, km, vtm, *lams, subln_gain_col)


def _decode_attn_kernel(pt_ref, q_ref, kn_ref, vn_ref, *rest):
    n = PAGES_PER_STEP
    k_refs = rest[:n]
    v_refs = rest[n:2 * n]
    lq1_ref, lk1_ref, lq2_ref, lk2_ref, sg_ref, o_ref, m_ref, l_ref, acc_ref = rest[2 * n:]
    step = pl.program_id(1)
    q = q_ref[0]
    n_rows = q.shape[0]
    n_new = n_rows // (2 * N_HEADS)

    def own_head(n_keys):
        row = lax.broadcasted_iota(jnp.int32, (n_rows, n_keys), 0)
        col = lax.broadcasted_iota(jnp.int32, (n_rows, n_keys), 1)
        return row, col, (col % N_HEADS) == (row // (2 * n_new))

    @pl.when(step == 0)
    def _():
        row, col, own = own_head(kn_ref.shape[1])
        s = jnp.where(own & ((col // N_HEADS) <= (row % n_new)), _dot_nt(q, kn_ref[0]), NEG_BIG)
        m = jnp.max(s, axis=1, keepdims=True)
        p = jnp.exp2(s - m)
        m_ref[...] = jnp.broadcast_to(m, m_ref.shape)
        l_ref[...] = jnp.broadcast_to(jnp.sum(p, axis=1, keepdims=True), l_ref.shape)
        acc_ref[...] = _dot(p.astype(BF16), vn_ref[0])

    _, _, own = own_head(PAGE_SIZE * N_HEADS)
    scores = [jnp.where(own, _dot_nt(q, k_refs[r][0].astype(BF16)), NEG_BIG) for r in range(n)]
    m_cur = scores[0]
    for s in scores[1:]:
        m_cur = jnp.maximum(m_cur, s)
    m_prev = m_ref[...]
    m_new = jnp.maximum(m_prev, jnp.max(m_cur, axis=1, keepdims=True))
    alpha = jnp.exp2(m_prev - m_new)
    l = alpha * l_ref[...]
    acc = alpha * acc_ref[...]
    for r in range(n):
        p = jnp.exp2(scores[r] - m_new[:, :1])
        l = l + jnp.sum(p, axis=1, keepdims=True)
        acc = acc + _dot(p.astype(BF16), v_refs[r][0].astype(BF16))
    m_ref[...] = m_new
    l_ref[...] = l
    acc_ref[...] = acc

    @pl.when(step == pl.num_programs(1) - 1)
    def _():
        lam = _lam(lq1_ref[...], lk1_ref[...], lq2_ref[...], lk2_ref[...])
        o = acc_ref[...] / l_ref[...]
        o = o - lam * pltpu.roll(o, n_rows - n_new, axis=0)
        o = _rms(o, sg_ref[...], SUBLN_EPS) * (1.0 - _lam_init(0))
        for h in range(N_HEADS):
            o_ref[0, :, h * V_DIM:(h + 1) * V_DIM] = o[h * 2 * n_new:h * 2 * n_new + n_new]


def _decode_attention(page_table, q_seq, k_new, v_new, cache_k, cache_v, lams, subln_gain):
    n_seq, n_pages = page_table.shape
    n_rows = q_seq.shape[1]
    n_new = n_rows // (2 * N_HEADS)
    page_rows = PAGE_SIZE * N_HEADS
    assert n_pages % PAGES_PER_STEP == 0 and cache_k.shape[1:] == (page_rows, V_DIM)
    page_spec = lambda r: pl.BlockSpec(
        (1, page_rows, V_DIM), lambda b, s, pt: (pt[b, s * PAGES_PER_STEP + r], 0, 0))
    seq_spec = lambda *shape: pl.BlockSpec((1,) + shape, lambda b, s, pt: (b,) + (0,) * len(shape))
    vec = lambda n: pl.BlockSpec((1, n), lambda b, s, pt: (0, 0))
    grid_spec = pltpu.PrefetchScalarGridSpec(
        num_scalar_prefetch=1,
        grid=(n_seq, n_pages // PAGES_PER_STEP),
        in_specs=[seq_spec(n_rows, V_DIM), seq_spec(*k_new.shape[1:]), seq_spec(*v_new.shape[1:])]
                 + [page_spec(r) for r in range(PAGES_PER_STEP)] * 2
                 + [vec(HEAD_DIM)] * 4 + [vec(V_DIM)],
        out_specs=seq_spec(n_new, D_ATTN),
        scratch_shapes=[pltpu.VMEM((n_rows, LANES), F32),
                        pltpu.VMEM((n_rows, LANES), F32),
                        pltpu.VMEM((n_rows, V_DIM), F32)],
    )
    return pl.pallas_call(
        _decode_attn_kernel,
        grid_spec=grid_spec,
        out_shape=jax.ShapeDtypeStruct((n_seq, n_new, D_ATTN), F32),
        compiler_params=pltpu.CompilerParams(
            dimension_semantics=("arbitrary", "arbitrary"), vmem_limit_bytes=VMEM_LIMIT),
        name="decode_attention",
    )(page_table, q_seq, k_new, v_new, *([cache_k] * PAGES_PER_STEP), *([cache_v] * PAGES_PER_STEP),
      *lams, subln_gain)


def _merge_tail(attn, pool_diffs, gate, x, wp_ref, ps_ref, wo_ref, fg_ref):
    pool = jnp.concatenate(
        [_dot(pool_diffs[g].astype(BF16), wp_ref[g]) for g in range(len(POOL_WINDOWS))], axis=1)
    mixed = jnp.concatenate([attn, pool * ps_ref[...]], axis=1)
    mixed = mixed * (gate * jax.nn.sigmoid(gate))
    h = x + _dot(mixed.astype(BF16), wo_ref[...])
    return _rms(h, fg_ref[...], RMS_EPS)


def _prompt_merge_kernel(attn_ref, u_ref, um_ref, gate_ref, x_ref, wp_ref, ps_ref, wo_ref, fg_ref,
                         y_ref, ubuf):
    rows = u_ref.shape[0]

    @pl.when(pl.program_id(0) == 0)
    def _():
        ubuf[0:HALO, :] = um_ref[...]

    ubuf[HALO:HALO + rows, :] = u_ref[...]
    diffs = []
    for g, w in enumerate(POOL_WINDOWS):
        sl = slice(g * POOL_GROUP, (g + 1) * POOL_GROUP)
        total = ubuf[HALO:HALO + rows, sl]
        for back in range(1, w):
            total = total + ubuf[HALO - back:HALO - back + rows, sl]
        diffs.append(total / float(w) - ubuf[HALO:HALO + rows, sl])
    y_ref[...] = _merge_tail(attn_ref[...], diffs, gate_ref[...], x_ref[...],
                             wp_ref, ps_ref, wo_ref, fg_ref)
    ubuf[0:HALO, :] = ubuf[rows:rows + HALO, :]


def _prompt_merge(attn, u, u_meta, gate, x, wp_b, pool_scale, wo_b, final_gain):
    rows = x.shape[0]
    t = MERGE_ROWS
    assert rows % t == 0 and u_meta.shape[0] == HALO
    row_spec = lambda cols: pl.BlockSpec((t, cols), lambda i: (i, 0))
    full = lambda *shape: pl.BlockSpec(shape, lambda i: (0,) * len(shape))
    return pl.pallas_call(
        _prompt_merge_kernel,
        grid=(rows // t,),
        in_specs=[row_spec(D_ATTN), row_spec(D_POOL), full(HALO, D_POOL), row_spec(D_MODEL),
                  row_spec(D_MODEL), full(len(POOL_WINDOWS), POOL_GROUP, POOL_GROUP),
                  full(1, D_POOL), full(D_MODEL, D_MODEL), full(1, D_MODEL)],
        out_specs=row_spec(D_MODEL),
        out_shape=jax.ShapeDtypeStruct((rows, D_MODEL), F32),
        scratch_shapes=[pltpu.VMEM((t + HALO, D_POOL), F32)],
        compiler_params=pltpu.CompilerParams(
            dimension_semantics=("arbitrary",), vmem_limit_bytes=VMEM_LIMIT),
        name="prompt_merge",
    )(attn, u, u_meta, gate, x, wp_b, pool_scale, wo_b, final_gain)


def _sample_merge_kernel(attn_ref, ush_ref, gate_ref, x_ref, wp_ref, ps_ref, wo_ref, fg_ref, y_ref):
    diffs = []
    for g, w in enumerate(POOL_WINDOWS):
        sl = slice(g * POOL_GROUP, (g + 1) * POOL_GROUP)
        total = ush_ref[0, :, sl]
        for back in range(1, w):
            total = total + ush_ref[back, :, sl]
        diffs.append(total / float(w) - ush_ref[0, :, sl])
    y_ref[...] = _merge_tail(attn_ref[...], diffs, gate_ref[...], x_ref[...],
                             wp_ref, ps_ref, wo_ref, fg_ref)


def _sample_merge(attn, u_shifted, gate, x, wp_b, pool_scale, wo_b, final_gain):
    rows = x.shape[0]
    return pl.pallas_call(
        _sample_merge_kernel,
        out_shape=jax.ShapeDtypeStruct((rows, D_MODEL), F32),
        compiler_params=pltpu.CompilerParams(vmem_limit_bytes=VMEM_LIMIT),
        name="sample_merge",
    )(attn, u_shifted, gate, x, wp_b, pool_scale, wo_b, final_gain)


def kernel(x_prompt, x_sample, cache_k, cache_v, state_pool, page_table, meta_tokens, norm_gain, w_in,
           lambda_q1, lambda_k1, lambda_q2, lambda_k2, subln_gain, w_pool, pool_scale, w_out, final_gain):
    batch, seq, _ = x_prompt.shape
    n_seq, n_new, _ = x_sample.shape
    n_samp = n_seq * n_new
    assert batch == 1 and cache_k.shape[0] == 1 and N_META + n_samp <= SMALL_ROWS
    p_len = page_table.shape[1] * PAGE_SIZE
    assert N_META + 1 >= max(POOL_WINDOWS) and p_len - POOL_STATE + 1 >= max(POOL_WINDOWS)

    w_n = w_in[0, :, D_ATTN:].astype(BF16)
    w_t = jnp.concatenate([w_in[0, :, :D_ATTN], w_in[0, :, 2 * D_ATTN:3 * D_ATTN]], axis=1).T.astype(BF16)
    w_out_b = w_out[0].astype(BF16)
    w_pool_b = w_pool[0].astype(BF16)
    lams = (lambda_q1, lambda_k1, lambda_q2, lambda_k2)
    final_gain2 = final_gain[None, :]

    qt_p, kf_p, kb_p, vf_p, vt_p, u_p, g_p = _project(
        x_prompt[0], _rope_tables(N_META + jnp.arange(seq)), norm_gain, w_n, w_t, PROJ_ROWS)
    n_pad = SMALL_ROWS - N_META - n_samp
    pos_small = jnp.concatenate([jnp.arange(N_META), jnp.tile(p_len + jnp.arange(n_new), n_seq),
                                 jnp.zeros((n_pad,), jnp.int32)])
    x_small = jnp.concatenate([meta_tokens, x_sample.reshape(n_samp, D_MODEL),
                               jnp.zeros((n_pad, D_MODEL), F32)], axis=0)
    qt_s, kf_s, kb_s, vf_s, vt_s, u_s, g_s = _project(
        x_small, _rope_tables(pos_small), norm_gain, w_n, w_t, SMALL_ROWS)
    samp = slice(N_META, N_META + n_samp)

    km = jnp.pad(kb_s[:, :N_META], ((0, 0), (0, LANES - N_META), (0, 0)))
    vtm = jnp.pad(vt_s[:, :, :N_META], ((0, 0), (0, 0), (0, LANES - N_META)))
    attn_p = _prompt_attention(qt_p, kb_p, vt_p, km, vtm, lams, subln_gain.reshape(V_DIM, 1))
    y_prompt = _prompt_merge(attn_p, u_p, u_s[:N_META], g_p, x_prompt[0], w_pool_b, pool_scale,
                             w_out_b, final_gain2)

    q_seq = qt_s[:, :, :, samp].reshape(N_HEADS, 2, V_DIM, n_seq, n_new)
    q_seq = q_seq.transpose(3, 0, 1, 4, 2).reshape(n_seq, N_HEADS * 2 * n_new, V_DIM)
    cache_rows = slice(N_META * N_HEADS, (N_META + n_samp) * N_HEADS)
    pad_new = ((0, 0), (0, LANES - n_new * N_HEADS), (0, 0))
    k_new = jnp.pad(kf_s[cache_rows].astype(BF16).reshape(n_seq, n_new * N_HEADS, V_DIM), pad_new)
    v_new = jnp.pad(vf_s[cache_rows].astype(BF16).reshape(n_seq, n_new * N_HEADS, V_DIM), pad_new)
    n_pool_pages = cache_k.shape[1]
    attn_s = _decode_attention(page_table, q_seq, k_new, v_new,
                               cache_k.reshape(n_pool_pages, PAGE_SIZE * N_HEADS, V_DIM),
                               cache_v.reshape(n_pool_pages, PAGE_SIZE * N_HEADS, V_DIM), lams, subln_gain)
    u_new = u_s[samp].reshape(n_seq, n_new, D_POOL)
    u_ext = jnp.concatenate([state_pool[0], u_new], axis=1)
    u_shifted = jnp.stack([u_ext[:, POOL_STATE - back:POOL_STATE - back + n_new].reshape(-1, D_POOL)
                           for back in range(max(POOL_WINDOWS))])
    y_sample = _sample_merge(attn_s.reshape(n_samp, D_ATTN), u_shifted, g_s[samp],
                             x_sample.reshape(n_samp, D_MODEL), w_pool_b, pool_scale,
                             w_out_b, final_gain2)

    l_p = seq + N_META
    n_meta_rows = N_META * N_HEADS
    new_k_prompt = jnp.concatenate([kf_s[:n_meta_rows], kf_p]).reshape(1, 1, l_p, N_HEADS, V_DIM)
    new_v_prompt = jnp.concatenate([vf_s[:n_meta_rows], vf_p]).reshape(1, 1, l_p, N_HEADS, V_DIM)
    new_pool_prompt = u_p[-POOL_STATE:].reshape(1, 1, POOL_STATE, D_POOL)
    new_k_sample = kf_s[cache_rows].reshape(1, n_seq, n_new, N_HEADS, V_DIM)
    new_v_sample = vf_s[cache_rows].reshape(1, n_seq, n_new, N_HEADS, V_DIM)
    new_pool_sample = u_ext[:, -POOL_STATE:][None]
    return (y_prompt[None], y_sample.reshape(n_seq, n_new, D_MODEL), new_k_prompt, new_v_prompt,
            new_pool_prompt, new_k_sample, new_v_sample, new_pool_sample)
```

```python
import math

import jax
import jax.numpy as jnp
from jax import lax
from jax.experimental import pallas as pl
from jax.experimental.pallas import tpu as pltpu

F32 = jnp.float32
BF16 = jnp.bfloat16

D_MODEL = 1024
N_META = 16
D_ATTN = 512
D_POOL = 512
HEAD_DIM = 64
N_HEADS = 4
V_DIM = 128
POOL_WINDOWS = (2, 4, 8, 16)
POOL_GROUP = 128
POOL_STATE = 15
D_IN = 3 * D_ATTN + D_POOL + D_MODEL
ROPE_THETA = 10000.0
RMS_EPS = 1e-6
SUBLN_EPS = 1e-5
PAGE_SIZE = 128
LANES = 128
HALO = 16
NEG_BIG = -1e30
Q_SCALE = HEAD_DIM ** -0.5 * math.log2(math.e)

PROJ_ROWS = 256
SMALL_ROWS = 256
ATTN_T = 512
MERGE_ROWS = 512
PAGES_PER_STEP = 16
VMEM_LIMIT = 48 * 1024 * 1024


def _dot(a, b):
    return jnp.dot(a, b, preferred_element_type=F32)


def _dot_nt(a, b):
    return lax.dot_general(a, b, (((1,), (1,)), ((), ())), preferred_element_type=F32)


def _lam_init(layer):
    return 0.8 - 0.6 * math.exp(-0.3 * layer)


def _lam(lq1, lk1, lq2, lk2):
    a = jnp.sum(lq1 * lk1, axis=1, keepdims=True)
    b = jnp.sum(lq2 * lk2, axis=1, keepdims=True)
    return jnp.exp(a) - jnp.exp(b) + _lam_init(0)


def _rms(x, gain, eps):
    return x * lax.rsqrt(jnp.mean(x * x, axis=-1, keepdims=True) + eps) * gain


def _project_kernel(x_ref, cos_ref, sin_ref, cost_ref, sint_ref, gain_ref, wn_ref, wt_ref,
                    qt_ref, kf_ref, kb_ref, vf_ref, vt_ref, u_ref, g_ref):
    rows = x_ref.shape[0]
    half = HEAD_DIM // 2
    xn = _rms(x_ref[...], gain_ref[...], RMS_EPS).astype(BF16)
    z = _dot(xn, wn_ref[...])
    zt = _dot_nt(wt_ref[...], xn)
    cos = cos_ref[...]
    sin = sin_ref[...]
    cost = cost_ref[...]
    sint = sint_ref[...]
    lane = lax.broadcasted_iota(jnp.int32, cos.shape, 1)
    first_half = (lane % HEAD_DIM) < half
    zeros = jnp.zeros((HEAD_DIM, rows), F32)

    for h in range(N_HEADS):
        for c in range(2):
            base = h * V_DIM + c * HEAD_DIM
            x1 = zt[base:base + half]
            x2 = zt[base + half:base + HEAD_DIM]
            o1 = (x1 * cost - x2 * sint) * Q_SCALE
            o2 = (x2 * cost + x1 * sint) * Q_SCALE
            parts = [o1, o2, zeros] if c == 0 else [zeros, o1, o2]
            qt_ref[h, c] = jnp.concatenate(parts, axis=0).astype(BF16)
        vt_ref[h] = zt[D_ATTN + h * V_DIM:D_ATTN + (h + 1) * V_DIM].astype(BF16)

        ks = z[:, h * V_DIM:(h + 1) * V_DIM]
        swapped = jnp.where(first_half, pltpu.roll(ks, LANES - half, axis=1), pltpu.roll(ks, half, axis=1))
        k = ks * cos + swapped * sin
        kf_ref[pl.ds(h, rows, stride=N_HEADS), :] = k
        kb_ref[h] = k.astype(BF16)
        vf_ref[pl.ds(h, rows, stride=N_HEADS), :] = z[:, D_ATTN + h * V_DIM:D_ATTN + (h + 1) * V_DIM]
    u_ref[...] = z[:, 2 * D_ATTN:2 * D_ATTN + D_POOL]
    g_ref[...] = z[:, 2 * D_ATTN + D_POOL:]


def _project(x, tables, gain, w_n, w_t, block_rows):
    rows = x.shape[0]
    assert rows % block_rows == 0
    cos, sin, cost, sint = tables
    row_spec = lambda cols: pl.BlockSpec((block_rows, cols), lambda i: (i, 0))
    col_spec = pl.BlockSpec((HEAD_DIM // 2, block_rows), lambda i: (0, i))
    full = lambda *shape: pl.BlockSpec(shape, lambda i: (0,) * len(shape))
    cache_spec = pl.BlockSpec((block_rows * N_HEADS, V_DIM), lambda i: (i, 0))
    return pl.pallas_call(
        _project_kernel,
        grid=(rows // block_rows,),
        in_specs=[row_spec(D_MODEL), row_spec(LANES), row_spec(LANES), col_spec, col_spec,
                  full(1, D_MODEL), full(*w_n.shape), full(*w_t.shape)],
        out_specs=[pl.BlockSpec((N_HEADS, 2, V_DIM, block_rows), lambda i: (0, 0, 0, i)),
                   cache_spec,
                   pl.BlockSpec((N_HEADS, block_rows, V_DIM), lambda i: (0, i, 0)),
                   cache_spec,
                   pl.BlockSpec((N_HEADS, V_DIM, block_rows), lambda i: (0, 0, i)),
                   row_spec(D_POOL), row_spec(D_MODEL)],
        out_shape=[jax.ShapeDtypeStruct((N_HEADS, 2, V_DIM, rows), BF16),
                   jax.ShapeDtypeStruct((rows * N_HEADS, V_DIM), F32),
                   jax.ShapeDtypeStruct((N_HEADS, rows, V_DIM), BF16),
                   jax.ShapeDtypeStruct((rows * N_HEADS, V_DIM), F32),
                   jax.ShapeDtypeStruct((N_HEADS, V_DIM, rows), BF16),
                   jax.ShapeDtypeStruct((rows, D_POOL), F32),
                   jax.ShapeDtypeStruct((rows, D_MODEL), F32)],
        compiler_params=pltpu.CompilerParams(
            dimension_semantics=("arbitrary",), vmem_limit_bytes=VMEM_LIMIT),
        name="project",
    )(x, cos, sin, cost, sint, gain, w_n, w_t)


def _rope_tables(pos):
    inv_freq = ROPE_THETA ** (-jnp.arange(0, HEAD_DIM, 2, dtype=F32) / HEAD_DIM)
    ang = pos.astype(F32)[:, None] * inv_freq[None, :]
    cos = jnp.cos(ang)
    sin = jnp.sin(ang)
    reps = LANES // HEAD_DIM
    return (jnp.tile(cos, (1, 2 * reps)), jnp.tile(jnp.concatenate([-sin, sin], axis=1), (1, reps)),
            cos.T, sin.T)


def _prompt_attn_kernel(qt_ref, k_ref, vt_ref, km_ref, vtm_ref,
                        lq1_ref, lk1_ref, lq2_ref, lk2_ref, sg_ref, o_ref,
                        m_ref, l_ref, acc_ref, sa_ref, sb_ref):
    i = pl.program_id(1)
    t = ATTN_T
    qt = jnp.concatenate([qt_ref[0, 0], qt_ref[0, 1]], axis=1)

    def scores(c):
        return _dot(k_ref[0, pl.ds(pl.multiple_of(c * t, t), t), :], qt)

    def values(c):
        return vt_ref[0, :, pl.ds(pl.multiple_of(c * t, t), t)]

    def update(s, vt):
        m_prev = m_ref[...]
        m_new = jnp.maximum(m_prev, jnp.max(s, axis=0, keepdims=True))
        alpha = jnp.exp2(m_prev - m_new)
        p = jnp.exp2(s - m_new)
        l_ref[...] = alpha * l_ref[...] + jnp.sum(p, axis=0, keepdims=True)
        acc_ref[...] = alpha * acc_ref[...] + _dot(vt, p.astype(BF16))
        m_ref[...] = m_new

    m_ref[...] = jnp.full(m_ref.shape, NEG_BIG, F32)
    l_ref[...] = jnp.zeros(l_ref.shape, F32)
    acc_ref[...] = jnp.zeros(acc_ref.shape, F32)

    sa_ref[...] = scores(0)

    key = lax.broadcasted_iota(jnp.int32, (LANES, 2 * t), 0)
    update(jnp.where(key < N_META, _dot(km_ref[0], qt), NEG_BIG), vtm_ref[0])

    def body(n, carry):
        c = 2 * n
        sb_ref[...] = scores(c + 1)
        update(sa_ref[...], values(c))
        sa_ref[...] = scores(c + 2)
        update(sb_ref[...], values(c + 1))
        return carry

    lax.fori_loop(0, i // 2, body, 0)

    key = lax.broadcasted_iota(jnp.int32, (t, 2 * t), 0)
    qry = lax.broadcasted_iota(jnp.int32, (t, 2 * t), 1) % t
    visible = key <= qry

    @pl.when(i % 2 == 0)
    def _():
        update(jnp.where(visible, sa_ref[...], NEG_BIG), values(i))

    @pl.when(i % 2 == 1)
    def _():
        sb_ref[...] = scores(i)
        update(sa_ref[...], values(i - 1))
        update(jnp.where(visible, sb_ref[...], NEG_BIG), values(i))

    lam = _lam(lq1_ref[...], lk1_ref[...], lq2_ref[...], lk2_ref[...])
    o = acc_ref[...] / l_ref[...]
    o = o[:, :t] - lam * o[:, t:]
    o = o * lax.rsqrt(jnp.mean(o * o, axis=0, keepdims=True) + SUBLN_EPS) * sg_ref[...]
    o_ref[...] = (o * (1.0 - _lam_init(0))).T


def _prompt_attention(qt, kb, vt, km, vtm, lams, subln_gain_col):
    rows = kb.shape[1]
    t = ATTN_T
    assert rows % t == 0
    vec = lambda n: pl.BlockSpec((1, n), lambda h, i: (0, 0))
    head = lambda *shape: pl.BlockSpec((1,) + shape, lambda h, i: (h,) + (0,) * len(shape))
    return pl.pallas_call(
        _prompt_attn_kernel,
        grid=(N_HEADS, rows // t),
        in_specs=[pl.BlockSpec((1, 2, V_DIM, t), lambda h, i: (h, 0, 0, i)),
                  head(rows, V_DIM), head(V_DIM, rows), head(LANES, V_DIM), head(V_DIM, LANES),
                  vec(HEAD_DIM), vec(HEAD_DIM), vec(HEAD_DIM), vec(HEAD_DIM),
                  pl.BlockSpec((V_DIM, 1), lambda h, i: (0, 0))],
        out_specs=pl.BlockSpec((t, V_DIM), lambda h, i: (i, h)),
        out_shape=jax.ShapeDtypeStruct((rows, D_ATTN), F32),
        scratch_shapes=[pltpu.VMEM((1, 2 * t), F32),
                        pltpu.VMEM((1, 2 * t), F32),
                        pltpu.VMEM((V_DIM, 2 * t), F32),
                        pltpu.VMEM((t, 2 * t), F32),
                        pltpu.VMEM((t, 2 * t), F32)],
        compiler_params=pltpu.CompilerParams(
            dimension_semantics=("arbitrary", "arbitrary"), vmem_limit_bytes=VMEM_LIMIT),
        name="prompt_attention",
    )(qt, kb, vt, km, vtm, *lams, subln_gain_col)


def _decode_attn_kernel(pt_ref, q_ref, kn_ref, vn_ref, *rest):
    n = PAGES_PER_STEP
    k_refs = rest[:n]
    v_refs = rest[n:2 * n]
    lq1_ref, lk1_ref, lq2_ref, lk2_ref, sg_ref, o_ref, m_ref, l_ref, acc_ref = rest[2 * n:]
    step = pl.program_id(1)
    q = q_ref[0]
    n_rows = q.shape[0]
    n_new = n_rows // (2 * N_HEADS)

    def own_head(n_keys):
        row = lax.broadcasted_iota(jnp.int32, (n_rows, n_keys), 0)
        col = lax.broadcasted_iota(jnp.int32, (n_rows, n_keys), 1)
        return row, col, (col % N_HEADS) == (row // (2 * n_new))

    @pl.when(step == 0)
    def _():
        row, col, own = own_head(kn_ref.shape[1])
        s = jnp.where(own & ((col // N_HEADS) <= (row % n_new)), _dot_nt(q, kn_ref[0]), NEG_BIG)
        m = jnp.max(s, axis=1, keepdims=True)
        p = jnp.exp2(s - m)
        m_ref[...] = jnp.broadcast_to(m, m_ref.shape)
        l_ref[...] = jnp.broadcast_to(jnp.sum(p, axis=1, keepdims=True), l_ref.shape)
        acc_ref[...] = _dot(p.astype(BF16), vn_ref[0])

    _, _, own = own_head(PAGE_SIZE * N_HEADS)
    scores = [jnp.where(own, _dot_nt(q, k_refs[r][0].astype(BF16)), NEG_BIG) for r in range(n)]
    m_cur = scores[0]
    for s in scores[1:]:
        m_cur = jnp.maximum(m_cur, s)
    m_prev = m_ref[...]
    m_new = jnp.maximum(m_prev, jnp.max(m_cur, axis=1, keepdims=True))
    alpha = jnp.exp2(m_prev - m_new)
    l = alpha * l_ref[...]
    acc = alpha * acc_ref[...]
    for r in range(n):
        p = jnp.exp2(scores[r] - m_new[:, :1])
        l = l + jnp.sum(p, axis=1, keepdims=True)
        acc = acc + _dot(p.astype(BF16), v_refs[r][0].astype(BF16))
    m_ref[...] = m_new
    l_ref[...] = l
    acc_ref[...] = acc

    @pl.when(step == pl.num_programs(1) - 1)
    def _():
        lam = _lam(lq1_ref[...], lk1_ref[...], lq2_ref[...], lk2_ref[...])
        o = acc_ref[...] / l_ref[...]
        o = o - lam * pltpu.roll(o, n_rows - n_new, axis=0)
        o = _rms(o, sg_ref[...], SUBLN_EPS) * (1.0 - _lam_init(0))
        for h in range(N_HEADS):
            o_ref[0, :, h * V_DIM:(h + 1) * V_DIM] = o[h * 2 * n_new:h * 2 * n_new + n_new]


def _decode_attention(page_table, q_seq, k_new, v_new, cache_k, cache_v, lams, subln_gain):
    n_seq, n_pages = page_table.shape
    n_rows = q_seq.shape[1]
    n_new = n_rows // (2 * N_HEADS)
    page_rows = PAGE_SIZE * N_HEADS
    assert n_pages % PAGES_PER_STEP == 0 and cache_k.shape[1:] == (page_rows, V_DIM)
    page_spec = lambda r: pl.BlockSpec(
        (1, page_rows, V_DIM), lambda b, s, pt: (pt[b, s * PAGES_PER_STEP + r], 0, 0))
    seq_spec = lambda *shape: pl.BlockSpec((1,) + shape, lambda b, s, pt: (b,) + (0,) * len(shape))
    vec = lambda n: pl.BlockSpec((1, n), lambda b, s, pt: (0, 0))
    grid_spec = pltpu.PrefetchScalarGridSpec(
        num_scalar_prefetch=1,
        grid=(n_seq, n_pages // PAGES_PER_STEP),
        in_specs=[seq_spec(n_rows, V_DIM), seq_spec(*k_new.shape[1:]), seq_spec(*v_new.shape[1:])]
                 + [page_spec(r) for r in range(PAGES_PER_STEP)] * 2
                 + [vec(HEAD_DIM)] * 4 + [vec(V_DIM)],
        out_specs=seq_spec(n_new, D_ATTN),
        scratch_shapes=[pltpu.VMEM((n_rows, LANES), F32),
                        pltpu.VMEM((n_rows, LANES), F32),
                        pltpu.VMEM((n_rows, V_DIM), F32)],
    )
    return pl.pallas_call(
        _decode_attn_kernel,
        grid_spec=grid_spec,
        out_shape=jax.ShapeDtypeStruct((n_seq, n_new, D_ATTN), F32),
        compiler_params=pltpu.CompilerParams(
            dimension_semantics=("arbitrary", "arbitrary"), vmem_limit_bytes=VMEM_LIMIT),
        name="decode_attention",
    )(page_table, q_seq, k_new, v_new, *([cache_k] * PAGES_PER_STEP), *([cache_v] * PAGES_PER_STEP),
      *lams, subln_gain)


def _merge_tail(attn, pool_diffs, gate, x, wp_ref, ps_ref, wo_ref, fg_ref):
    pool = jnp.concatenate(
        [_dot(pool_diffs[g].astype(BF16), wp_ref[g]) for g in range(len(POOL_WINDOWS))], axis=1)
    mixed = jnp.concatenate([attn, pool * ps_ref[...]], axis=1)
    mixed = mixed * (gate * jax.nn.sigmoid(gate))
    h = x + _dot(mixed.astype(BF16), wo_ref[...])
    return _rms(h, fg_ref[...], RMS_EPS)


def _prompt_merge_kernel(attn_ref, u_ref, um_ref, gate_ref, x_ref, wp_ref, ps_ref, wo_ref, fg_ref,
                         y_ref, ubuf):
    rows = u_ref.shape[0]

    @pl.when(pl.program_id(0) == 0)
    def _():
        ubuf[0:HALO, :] = um_ref[...]

    ubuf[HALO:HALO + rows, :] = u_ref[...]
    diffs = []
    for g, w in enumerate(POOL_WINDOWS):
        sl = slice(g * POOL_GROUP, (g + 1) * POOL_GROUP)
        total = ubuf[HALO:HALO + rows, sl]
        for back in range(1, w):
            total = total + ubuf[HALO - back:HALO - back + rows, sl]
        diffs.append(total / float(w) - ubuf[HALO:HALO + rows, sl])
    y_ref[...] = _merge_tail(attn_ref[...], diffs, gate_ref[...], x_ref[...],
                             wp_ref, ps_ref, wo_ref, fg_ref)
    ubuf[0:HALO, :] = ubuf[rows:rows + HALO, :]


def _prompt_merge(attn, u, u_meta, gate, x, wp_b, pool_scale, wo_b, final_gain):
    rows = x.shape[0]
    t = MERGE_ROWS
    assert rows % t == 0 and u_meta.shape[0] == HALO
    row_spec = lambda cols: pl.BlockSpec((t, cols), lambda i: (i, 0))
    full = lambda *shape: pl.BlockSpec(shape, lambda i: (0,) * len(shape))
    return pl.pallas_call(
        _prompt_merge_kernel,
        grid=(rows // t,),
        in_specs=[row_spec(D_ATTN), row_spec(D_POOL), full(HALO, D_POOL), row_spec(D_MODEL),
                  row_spec(D_MODEL), full(len(POOL_WINDOWS), POOL_GROUP, POOL_GROUP),
                  full(1, D_POOL), full(D_MODEL, D_MODEL), full(1, D_MODEL)],
        out_specs=row_spec(D_MODEL),
        out_shape=jax.ShapeDtypeStruct((rows, D_MODEL), F32),
        scratch_shapes=[pltpu.VMEM((t + HALO, D_POOL), F32)],
        compiler_params=pltpu.CompilerParams(
            dimension_semantics=("arbitrary",), vmem_limit_bytes=VMEM_LIMIT),
        name="prompt_merge",
    )(attn, u, u_meta, gate, x, wp_b, pool_scale, wo_b, final_gain)


def _sample_merge_kernel(attn_ref, ush_ref, gate_ref, x_ref, wp_ref, ps_ref, wo_ref, fg_ref, y_ref):
    diffs = []
    for g, w in enumerate(POOL_WINDOWS):
        sl = slice(g * POOL_GROUP, (g + 1) * POOL_GROUP)
        total = ush_ref[0, :, sl]
        for back in range(1, w):
            total = total + ush_ref[back, :, sl]
        diffs.append(total / float(w) - ush_ref[0, :, sl])
    y_ref[...] = _merge_tail(attn_ref[...], diffs, gate_ref[...], x_ref[...],
                             wp_ref, ps_ref, wo_ref, fg_ref)


def _sample_merge(attn, u_shifted, gate, x, wp_b, pool_scale, wo_b, final_gain):
    rows = x.shape[0]
    return pl.pallas_call(
        _sample_merge_kernel,
        out_shape=jax.ShapeDtypeStruct((rows, D_MODEL), F32),
        compiler_params=pltpu.CompilerParams(vmem_limit_bytes=VMEM_LIMIT),
        name="sample_merge",
    )(attn, u_shifted, gate, x, wp_b, pool_scale, wo_b, final_gain)


def kernel(x_prompt, x_sample, cache_k, cache_v, state_pool, page_table, meta_tokens, norm_gain, w_in,
           lambda_q1, lambda_k1, lambda_q2, lambda_k2, subln_gain, w_pool, pool_scale, w_out, final_gain):
    batch, seq, _ = x_prompt.shape
    n_seq, n_new, _ = x_sample.shape
    n_samp = n_seq * n_new
    assert batch == 1 and cache_k.shape[0] == 1 and N_META + n_samp <= SMALL_ROWS
    p_len = page_table.shape[1] * PAGE_SIZE
    assert N_META + 1 >= max(POOL_WINDOWS) and p_len - POOL_STATE + 1 >= max(POOL_WINDOWS)

    w_n = w_in[0, :, D_ATTN:].astype(BF16)
    w_t = jnp.concatenate([w_in[0, :, :D_ATTN], w_in[0, :, 2 * D_ATTN:3 * D_ATTN]], axis=1).T.astype(BF16)
    w_out_b = w_out[0].astype(BF16)
    w_pool_b = w_pool[0].astype(BF16)
    lams = (lambda_q1, lambda_k1, lambda_q2, lambda_k2)
    final_gain2 = final_gain[None, :]

    qt_p, kf_p, kb_p, vf_p, vt_p, u_p, g_p = _project(
        x_prompt[0], _rope_tables(N_META + jnp.arange(seq)), norm_gain, w_n, w_t, PROJ_ROWS)
    n_pad = SMALL_ROWS - N_META - n_samp
    pos_small = jnp.concatenate([jnp.arange(N_META), jnp.tile(p_len + jnp.arange(n_new), n_seq),
                                 jnp.zeros((n_pad,), jnp.int32)])
    x_small = jnp.concatenate([meta_tokens, x_sample.reshape(n_samp, D_MODEL),
                               jnp.zeros((n_pad, D_MODEL), F32)], axis=0)
    qt_s, kf_s, kb_s, vf_s, vt_s, u_s, g_s = _project(
        x_small, _rope_tables(pos_small), norm_gain, w_n, w_t, SMALL_ROWS)
    samp = slice(N_META, N_META + n_samp)

    km = jnp.pad(kb_s[:, :N_META], ((0, 0), (0, LANES - N_META), (0, 0)))
    vtm = jnp.pad(vt_s[:, :, :N_META], ((0, 0), (0, 0), (0, LANES - N_META)))
    attn_p = _prompt_attention(qt_p, kb_p, vt_p, km, vtm, lams, subln_gain.reshape(V_DIM, 1))
    y_prompt = _prompt_merge(attn_p, u_p, u_s[:N_META], g_p, x_prompt[0], w_pool_b, pool_scale,
                             w_out_b, final_gain2)

    q_seq = qt_s[:, :, :, samp].reshape(N_HEADS, 2, V_DIM, n_seq, n_new)
    q_seq = q_seq.transpose(3, 0, 1, 4, 2).reshape(n_seq, N_HEADS * 2 * n_new, V_DIM)
    cache_rows = slice(N_META * N_HEADS, (N_META + n_samp) * N_HEADS)
    pad_new = ((0, 0), (0, LANES - n_new * N_HEADS), (0, 0))
    k_new = jnp.pad(kf_s[cache_rows].astype(BF16).reshape(n_seq, n_new * N_HEADS, V_DIM), pad_new)
    v_new = jnp.pad(vf_s[cache_rows].astype(BF16).reshape(n_seq, n_new * N_HEADS, V_DIM), pad_new)
    n_pool_pages = cache_k.shape[1]
    attn_s = _decode_attention(page_table, q_seq, k_new, v_new,
                               cache_k.reshape(n_pool_pages, PAGE_SIZE * N_HEADS, V_DIM),
                               cache_v.reshape(n_pool_pages, PAGE_SIZE * N_HEADS, V_DIM), lams, subln_gain)
    u_new = u_s[samp].reshape(n_seq, n_new, D_POOL)
    u_ext = jnp.concatenate([state_pool[0], u_new], axis=1)
    u_shifted = jnp.stack([u_ext[:, POOL_STATE - back:POOL_STATE - back + n_new].reshape(-1, D_POOL)
                           for back in range(max(POOL_WINDOWS))])
    y_sample = _sample_merge(attn_s.reshape(n_samp, D_ATTN), u_shifted, g_s[samp],
                             x_sample.reshape(n_samp, D_MODEL), w_pool_b, pool_scale,
                             w_out_b, final_gain2)

    l_p = seq + N_META
    n_meta_rows = N_META * N_HEADS
    new_k_prompt = jnp.concatenate([kf_s[:n_meta_rows], kf_p]).reshape(1, 1, l_p, N_HEADS, V_DIM)
    new_v_prompt = jnp.concatenate([vf_s[:n_meta_rows], vf_p]).reshape(1, 1, l_p, N_HEADS, V_DIM)
    new_pool_prompt = u_p[-POOL_STATE:].reshape(1, 1, POOL_STATE, D_POOL)
    new_k_sample = kf_s[cache_rows].reshape(1, n_seq, n_new, N_HEADS, V_DIM)
    new_v_sample = vf_s[cache_rows].reshape(1, n_seq, n_new, N_HEADS, V_DIM)
    new_pool_sample = u_ext[:, -POOL_STATE:][None]
    return (y_prompt[None], y_sample.reshape(n_seq, n_new, D_MODEL), new_k_prompt, new_v_prompt,
            new_pool_prompt, new_k_sample, new_v_sample, new_pool_sample)
```

```python
import math

import jax
import jax.numpy as jnp
from jax import lax
from jax.experimental import pallas as pl
from jax.experimental.pallas import tpu as pltpu

F32 = jnp.float32
BF16 = jnp.bfloat16

D_MODEL = 1024
N_META = 16
D_ATTN = 512
D_POOL = 512
HEAD_DIM = 64
N_HEADS = 4
V_DIM = 128
POOL_WINDOWS = (2, 4, 8, 16)
POOL_GROUP = 128
POOL_STATE = 15
D_IN = 3 * D_ATTN + D_POOL + D_MODEL
ROPE_THETA = 10000.0
RMS_EPS = 1e-6
SUBLN_EPS = 1e-5
PAGE_SIZE = 128
LANES = 128
HALO = 16
NEG_BIG = -1e30
Q_SCALE = HEAD_DIM ** -0.5 * math.log2(math.e)

PROJ_ROWS = 256
SMALL_ROWS = 256
ATTN_T = 512
SUM_ROWS = 16
VT_ROWS = V_DIM + SUM_ROWS
MERGE_ROWS = 512
PAGES_PER_STEP = 16
PAGE_GROUP = 4
VMEM_LIMIT = 48 * 1024 * 1024


def _dot(a, b):
    return jnp.dot(a, b, preferred_element_type=F32)


def _dot_nt(a, b):
    return lax.dot_general(a, b, (((1,), (1,)), ((), ())), preferred_element_type=F32)


def _lam_init(layer):
    return 0.8 - 0.6 * math.exp(-0.3 * layer)


def _lam(lq1, lk1, lq2, lk2):
    a = jnp.sum(lq1 * lk1, axis=1, keepdims=True)
    b = jnp.sum(lq2 * lk2, axis=1, keepdims=True)
    return jnp.exp(a) - jnp.exp(b) + _lam_init(0)


def _rms(x, gain, eps):
    return x * lax.rsqrt(jnp.mean(x * x, axis=-1, keepdims=True) + eps) * gain


def _project_kernel(x_ref, cos_ref, sin_ref, cost_ref, sint_ref, gain_ref, wn_ref, wt_ref,
                    qt_ref, kf_ref, kb_ref, vf_ref, vt_ref, u_ref, g_ref):
    rows = x_ref.shape[0]
    half = HEAD_DIM // 2
    xn = _rms(x_ref[...], gain_ref[...], RMS_EPS).astype(BF16)
    z = _dot(xn, wn_ref[...])
    zt = _dot_nt(wt_ref[...], xn)
    cos = cos_ref[...]
    sin = sin_ref[...]
    cost = cost_ref[...]
    sint = sint_ref[...]
    lane = lax.broadcasted_iota(jnp.int32, cos.shape, 1)
    first_half = (lane % HEAD_DIM) < half
    zeros = jnp.zeros((HEAD_DIM, rows), F32)

    for h in range(N_HEADS):
        for c in range(2):
            base = h * V_DIM + c * HEAD_DIM
            x1 = zt[base:base + half]
            x2 = zt[base + half:base + HEAD_DIM]
            o1 = (x1 * cost - x2 * sint) * Q_SCALE
            o2 = (x2 * cost + x1 * sint) * Q_SCALE
            parts = [o1, o2, zeros] if c == 0 else [zeros, o1, o2]
            qt_ref[h, c] = jnp.concatenate(parts, axis=0).astype(BF16)
        vt_ref[h, :V_DIM] = zt[D_ATTN + h * V_DIM:D_ATTN + (h + 1) * V_DIM].astype(BF16)
        vt_ref[h, V_DIM:] = jnp.ones((SUM_ROWS, rows), BF16)

        ks = z[:, h * V_DIM:(h + 1) * V_DIM]
        swapped = jnp.where(first_half, pltpu.roll(ks, LANES - half, axis=1), pltpu.roll(ks, half, axis=1))
        k = ks * cos + swapped * sin
        kf_ref[pl.ds(h, rows, stride=N_HEADS), :] = k
        kb_ref[h] = k.astype(BF16)
        vf_ref[pl.ds(h, rows, stride=N_HEADS), :] = z[:, D_ATTN + h * V_DIM:D_ATTN + (h + 1) * V_DIM]
    u_ref[...] = z[:, 2 * D_ATTN:2 * D_ATTN + D_POOL]
    g_ref[...] = z[:, 2 * D_ATTN + D_POOL:]


def _project(x, tables, gain, w_n, w_t, block_rows):
    rows = x.shape[0]
    assert rows % block_rows == 0
    cos, sin, cost, sint = tables
    row_spec = lambda cols: pl.BlockSpec((block_rows, cols), lambda i: (i, 0))
    col_spec = pl.BlockSpec((HEAD_DIM // 2, block_rows), lambda i: (0, i))
    full = lambda *shape: pl.BlockSpec(shape, lambda i: (0,) * len(shape))
    cache_spec = pl.BlockSpec((block_rows * N_HEADS, V_DIM), lambda i: (i, 0))
    return pl.pallas_call(
        _project_kernel,
        grid=(rows // block_rows,),
        in_specs=[row_spec(D_MODEL), row_spec(LANES), row_spec(LANES), col_spec, col_spec,
                  full(1, D_MODEL), full(*w_n.shape), full(*w_t.shape)],
        out_specs=[pl.BlockSpec((N_HEADS, 2, V_DIM, block_rows), lambda i: (0, 0, 0, i)),
                   cache_spec,
                   pl.BlockSpec((N_HEADS, block_rows, V_DIM), lambda i: (0, i, 0)),
                   cache_spec,
                   pl.BlockSpec((N_HEADS, VT_ROWS, block_rows), lambda i: (0, 0, i)),
                   row_spec(D_POOL), row_spec(D_MODEL)],
        out_shape=[jax.ShapeDtypeStruct((N_HEADS, 2, V_DIM, rows), BF16),
                   jax.ShapeDtypeStruct((rows * N_HEADS, V_DIM), F32),
                   jax.ShapeDtypeStruct((N_HEADS, rows, V_DIM), BF16),
                   jax.ShapeDtypeStruct((rows * N_HEADS, V_DIM), F32),
                   jax.ShapeDtypeStruct((N_HEADS, VT_ROWS, rows), BF16),
                   jax.ShapeDtypeStruct((rows, D_POOL), F32),
                   jax.ShapeDtypeStruct((rows, D_MODEL), F32)],
        compiler_params=pltpu.CompilerParams(
            dimension_semantics=("arbitrary",), vmem_limit_bytes=VMEM_LIMIT),
        name="project",
    )(x, cos, sin, cost, sint, gain, w_n, w_t)


def _cos_sin(pos):
    inv_freq = ROPE_THETA ** (-jnp.arange(0, HEAD_DIM, 2, dtype=F32) / HEAD_DIM)
    ang = pos.astype(F32)[:, None] * inv_freq[None, :]
    return jnp.cos(ang), jnp.sin(ang)


def _cos_sin_range(start, n, block=256):
    assert n % block == 0
    cos_a, sin_a = _cos_sin(start + block * jnp.arange(n // block))
    cos_b, sin_b = _cos_sin(jnp.arange(block))
    cos = cos_a[:, None, :] * cos_b[None] - sin_a[:, None, :] * sin_b[None]
    sin = sin_a[:, None, :] * cos_b[None] + cos_a[:, None, :] * sin_b[None]
    return cos.reshape(n, -1), sin.reshape(n, -1)


def _rope_tables(cos, sin):
    reps = LANES // HEAD_DIM
    return (jnp.tile(cos, (1, 2 * reps)), jnp.tile(jnp.concatenate([-sin, sin], axis=1), (1, reps)),
            cos.T, sin.T)


def _prompt_attn_kernel(qt_ref, k_ref, vt_ref, km_ref, vtm_ref,
                        lq1_ref, lk1_ref, lq2_ref, lk2_ref, sg_ref, o_ref,
                        m_ref, acc_ref, sa_ref, sb_ref):
    i = pl.program_id(1)
    t = ATTN_T
    qt = jnp.concatenate([qt_ref[0, 0], qt_ref[0, 1]], axis=1)

    def scores(c):
        return _dot(k_ref[0, pl.ds(pl.multiple_of(c * t, t), t), :], qt)

    def values(c):
        return vt_ref[0, :, pl.ds(pl.multiple_of(c * t, t), t)]

    def update(s, vt):
        m_prev = m_ref[...]
        m_new = jnp.maximum(m_prev, jnp.max(s, axis=0, keepdims=True))
        alpha = jnp.exp2(m_prev - m_new)
        p = jnp.exp2(s - m_new).astype(BF16)
        acc_ref[...] = alpha * acc_ref[...] + _dot(vt, p)
        m_ref[...] = m_new

    m_ref[...] = jnp.full(m_ref.shape, NEG_BIG, F32)
    acc_ref[...] = jnp.zeros(acc_ref.shape, F32)

    sa_ref[...] = scores(0)

    key = lax.broadcasted_iota(jnp.int32, (LANES, 2 * t), 0)
    update(jnp.where(key < N_META, _dot(km_ref[0], qt), NEG_BIG), vtm_ref[0])

    def body(n, carry):
        c = 2 * n
        sb_ref[...] = scores(c + 1)
        update(sa_ref[...], values(c))
        sa_ref[...] = scores(c + 2)
        update(sb_ref[...], values(c + 1))
        return carry

    lax.fori_loop(0, i // 2, body, 0)

    key = lax.broadcasted_iota(jnp.int32, (t, 2 * t), 0)
    qry = lax.broadcasted_iota(jnp.int32, (t, 2 * t), 1) % t
    visible = key <= qry

    @pl.when(i % 2 == 0)
    def _():
        update(jnp.where(visible, sa_ref[...], NEG_BIG), values(i))

    @pl.when(i % 2 == 1)
    def _():
        sb_ref[...] = scores(i)
        update(sa_ref[...], values(i - 1))
        update(jnp.where(visible, sb_ref[...], NEG_BIG), values(i))

    lam = _lam(lq1_ref[...], lk1_ref[...], lq2_ref[...], lk2_ref[...])
    o = acc_ref[:V_DIM, :] / acc_ref[V_DIM:V_DIM + 1, :]
    o = o[:, :t] - lam * o[:, t:]
    o = o * lax.rsqrt(jnp.mean(o * o, axis=0, keepdims=True) + SUBLN_EPS) * sg_ref[...]
    o_ref[...] = (o * (1.0 - _lam_init(0))).T


def _prompt_attention(qt, kb, vt, km, vtm, lams, subln_gain_col):
    rows = kb.shape[1]
    t = ATTN_T
    assert rows % t == 0
    vec = lambda n: pl.BlockSpec((1, n), lambda h, i: (0, 0))
    head = lambda *shape: pl.BlockSpec((1,) + shape, lambda h, i: (h,) + (0,) * len(shape))
    return pl.pallas_call(
        _prompt_attn_kernel,
        grid=(N_HEADS, rows // t),
        in_specs=[pl.BlockSpec((1, 2, V_DIM, t), lambda h, i: (h, 0, 0, i)),
                  head(rows, V_DIM), head(VT_ROWS, rows), head(LANES, V_DIM), head(VT_ROWS, LANES),
                  vec(HEAD_DIM), vec(HEAD_DIM), vec(HEAD_DIM), vec(HEAD_DIM),
                  pl.BlockSpec((V_DIM, 1), lambda h, i: (0, 0))],
        out_specs=pl.BlockSpec((t, V_DIM), lambda h, i: (i, h)),
        out_shape=jax.ShapeDtypeStruct((rows, D_ATTN), F32),
        scratch_shapes=[pltpu.VMEM((1, 2 * t), F32),
                        pltpu.VMEM((VT_ROWS, 2 * t), F32),
                        pltpu.VMEM((t, 2 * t), F32),
                        pltpu.VMEM((t, 2 * t), F32)],
        compiler_params=pltpu.CompilerParams(
            dimension_semantics=("arbitrary", "arbitrary"), vmem_limit_bytes=VMEM_LIMIT),
        name="prompt_attention",
    )(qt, kb, vt, km, vtm, *lams, subln_gain_col)


def _decode_attn_kernel(pt_ref, q_ref, kn_ref, vn_ref, *rest):
    n = PAGES_PER_STEP
    k_refs = rest[:n]
    v_refs = rest[n:2 * n]
    lq1_ref, lk1_ref, lq2_ref, lk2_ref, sg_ref, o_ref, m_ref, l_ref, acc_ref = rest[2 * n:]
    step = pl.program_id(1)
    q = q_ref[0]
    n_rows = q.shape[0]
    n_new = n_rows // (2 * N_HEADS)

    def own_head(n_keys):
        row = lax.broadcasted_iota(jnp.int32, (n_rows, n_keys), 0)
        col = lax.broadcasted_iota(jnp.int32, (n_rows, n_keys), 1)
        return row, col, (col % N_HEADS) == (row // (2 * n_new))

    @pl.when(step == 0)
    def _():
        row, col, own = own_head(kn_ref.shape[1])
        s = jnp.where(own & ((col // N_HEADS) <= (row % n_new)), _dot_nt(q, kn_ref[0]), NEG_BIG)
        m = jnp.max(s, axis=1, keepdims=True)
        p = jnp.exp2(s - m)
        m_ref[...] = jnp.broadcast_to(m, m_ref.shape)
        l_ref[...] = jnp.broadcast_to(jnp.sum(p, axis=1, keepdims=True), l_ref.shape)
        acc_ref[...] = _dot(p.astype(BF16), vn_ref[0])

    _, _, own = own_head(PAGE_SIZE * N_HEADS)
    scores = [jnp.where(own, _dot_nt(q, k_refs[r][0].astype(BF16)), NEG_BIG) for r in range(n)]
    m_prev = m_ref[...]
    l = l_ref[...]
    acc = acc_ref[...]
    for g in range(0, n, PAGE_GROUP):
        m_cur = scores[g]
        for s in scores[g + 1:g + PAGE_GROUP]:
            m_cur = jnp.maximum(m_cur, s)
        m_new = jnp.maximum(m_prev, jnp.max(m_cur, axis=1, keepdims=True))
        alpha = jnp.exp2(m_prev - m_new)
        l = alpha * l
        acc = alpha * acc
        for r in range(g, g + PAGE_GROUP):
            p = jnp.exp2(scores[r] - m_new[:, :1])
            l = l + jnp.sum(p, axis=1, keepdims=True)
            acc = acc + _dot(p.astype(BF16), v_refs[r][0].astype(BF16))
        m_prev = m_new
    m_ref[...] = m_prev
    l_ref[...] = l
    acc_ref[...] = acc

    @pl.when(step == pl.num_programs(1) - 1)
    def _():
        lam = _lam(lq1_ref[...], lk1_ref[...], lq2_ref[...], lk2_ref[...])
        o = acc_ref[...] / l_ref[...]
        o = o - lam * pltpu.roll(o, n_rows - n_new, axis=0)
        o = _rms(o, sg_ref[...], SUBLN_EPS) * (1.0 - _lam_init(0))
        for h in range(N_HEADS):
            o_ref[0, :, h * V_DIM:(h + 1) * V_DIM] = o[h * 2 * n_new:h * 2 * n_new + n_new]


def _decode_attention(page_table, q_seq, k_new, v_new, cache_k, cache_v, lams, subln_gain):
    n_seq, n_pages = page_table.shape
    n_rows = q_seq.shape[1]
    n_new = n_rows // (2 * N_HEADS)
    page_rows = PAGE_SIZE * N_HEADS
    assert n_pages % PAGES_PER_STEP == 0 and cache_k.shape[1:] == (page_rows, V_DIM)
    page_spec = lambda r: pl.BlockSpec(
        (1, page_rows, V_DIM), lambda b, s, pt: (pt[b, s * PAGES_PER_STEP + r], 0, 0))
    seq_spec = lambda *shape: pl.BlockSpec((1,) + shape, lambda b, s, pt: (b,) + (0,) * len(shape))
    vec = lambda n: pl.BlockSpec((1, n), lambda b, s, pt: (0, 0))
    grid_spec = pltpu.PrefetchScalarGridSpec(
        num_scalar_prefetch=1,
        grid=(n_seq, n_pages // PAGES_PER_STEP),
        in_specs=[seq_spec(n_rows, V_DIM), seq_spec(*k_new.shape[1:]), seq_spec(*v_new.shape[1:])]
                 + [page_spec(r) for r in range(PAGES_PER_STEP)] * 2
                 + [vec(HEAD_DIM)] * 4 + [vec(V_DIM)],
        out_specs=seq_spec(n_new, D_ATTN),
        scratch_shapes=[pltpu.VMEM((n_rows, LANES), F32),
                        pltpu.VMEM((n_rows, LANES), F32),
                        pltpu.VMEM((n_rows, V_DIM), F32)],
    )
    return pl.pallas_call(
        _decode_attn_kernel,
        grid_spec=grid_spec,
        out_shape=jax.ShapeDtypeStruct((n_seq, n_new, D_ATTN), F32),
        compiler_params=pltpu.CompilerParams(
            dimension_semantics=("arbitrary", "arbitrary"), vmem_limit_bytes=VMEM_LIMIT),
        name="decode_attention",
    )(page_table, q_seq, k_new, v_new, *([cache_k] * PAGES_PER_STEP), *([cache_v] * PAGES_PER_STEP),
      *lams, subln_gain)


def _merge_tail(attn, pool_diffs, gate, x, wp_ref, ps_ref, wo_ref, fg_ref):
    pool = jnp.concatenate(
        [_dot(pool_diffs[g].astype(BF16), wp_ref[g]) for g in range(len(POOL_WINDOWS))], axis=1)
    mixed = jnp.concatenate([attn, pool * ps_ref[...]], axis=1)
    mixed = mixed * (gate * jax.nn.sigmoid(gate))
    h = x + _dot(mixed.astype(BF16), wo_ref[...])
    return _rms(h, fg_ref[...], RMS_EPS)


def _prompt_merge_kernel(attn_ref, u_ref, um_ref, gate_ref, x_ref, wp_ref, ps_ref, wo_ref, fg_ref,
                         y_ref, ubuf):
    rows = u_ref.shape[0]

    @pl.when(pl.program_id(0) == 0)
    def _():
        ubuf[0:HALO, :] = um_ref[...]

    ubuf[HALO:HALO + rows, :] = u_ref[...]
    diffs = []
    for g, w in enumerate(POOL_WINDOWS):
        sl = slice(g * POOL_GROUP, (g + 1) * POOL_GROUP)
        total = ubuf[HALO:HALO + rows, sl]
        for back in range(1, w):
            total = total + ubuf[HALO - back:HALO - back + rows, sl]
        diffs.append(total / float(w) - ubuf[HALO:HALO + rows, sl])
    y_ref[...] = _merge_tail(attn_ref[...], diffs, gate_ref[...], x_ref[...],
                             wp_ref, ps_ref, wo_ref, fg_ref)
    ubuf[0:HALO, :] = ubuf[rows:rows + HALO, :]


def _prompt_merge(attn, u, u_meta, gate, x, wp_b, pool_scale, wo_b, final_gain):
    rows = x.shape[0]
    t = MERGE_ROWS
    assert rows % t == 0 and u_meta.shape[0] == HALO
    row_spec = lambda cols: pl.BlockSpec((t, cols), lambda i: (i, 0))
    full = lambda *shape: pl.BlockSpec(shape, lambda i: (0,) * len(shape))
    return pl.pallas_call(
        _prompt_merge_kernel,
        grid=(rows // t,),
        in_specs=[row_spec(D_ATTN), row_spec(D_POOL), full(HALO, D_POOL), row_spec(D_MODEL),
                  row_spec(D_MODEL), full(len(POOL_WINDOWS), POOL_GROUP, POOL_GROUP),
                  full(1, D_POOL), full(D_MODEL, D_MODEL), full(1, D_MODEL)],
        out_specs=row_spec(D_MODEL),
        out_shape=jax.ShapeDtypeStruct((rows, D_MODEL), F32),
        scratch_shapes=[pltpu.VMEM((t + HALO, D_POOL), F32)],
        compiler_params=pltpu.CompilerParams(
            dimension_semantics=("arbitrary",), vmem_limit_bytes=VMEM_LIMIT),
        name="prompt_merge",
    )(attn, u, u_meta, gate, x, wp_b, pool_scale, wo_b, final_gain)


def _sample_merge_kernel(attn_ref, ush_ref, gate_ref, x_ref, wp_ref, ps_ref, wo_ref, fg_ref, y_ref):
    diffs = []
    for g, w in enumerate(POOL_WINDOWS):
        sl = slice(g * POOL_GROUP, (g + 1) * POOL_GROUP)
        total = ush_ref[0, :, sl]
        for back in range(1, w):
            total = total + ush_ref[back, :, sl]
        diffs.append(total / float(w) - ush_ref[0, :, sl])
    y_ref[...] = _merge_tail(attn_ref[...], diffs, gate_ref[...], x_ref[...],
                             wp_ref, ps_ref, wo_ref, fg_ref)


def _sample_merge(attn, u_shifted, gate, x, wp_b, pool_scale, wo_b, final_gain):
    rows = x.shape[0]
    return pl.pallas_call(
        _sample_merge_kernel,
        out_shape=jax.ShapeDtypeStruct((rows, D_MODEL), F32),
        compiler_params=pltpu.CompilerParams(vmem_limit_bytes=VMEM_LIMIT),
        name="sample_merge",
    )(attn, u_shifted, gate, x, wp_b, pool_scale, wo_b, final_gain)


def kernel(x_prompt, x_sample, cache_k, cache_v, state_pool, page_table, meta_tokens, norm_gain, w_in,
           lambda_q1, lambda_k1, lambda_q2, lambda_k2, subln_gain, w_pool, pool_scale, w_out, final_gain):
    batch, seq, _ = x_prompt.shape
    n_seq, n_new, _ = x_sample.shape
    n_samp = n_seq * n_new
    assert batch == 1 and cache_k.shape[0] == 1 and N_META + n_samp <= SMALL_ROWS
    p_len = page_table.shape[1] * PAGE_SIZE
    assert N_META + 1 >= max(POOL_WINDOWS) and p_len - POOL_STATE + 1 >= max(POOL_WINDOWS)

    w_n = w_in[0, :, D_ATTN:].astype(BF16)
    w_t = jnp.concatenate([w_in[0, :, :D_ATTN], w_in[0, :, 2 * D_ATTN:3 * D_ATTN]], axis=1).T.astype(BF16)
    w_out_b = w_out[0].astype(BF16)
    w_pool_b = w_pool[0].astype(BF16)
    lams = (lambda_q1, lambda_k1, lambda_q2, lambda_k2)
    final_gain2 = final_gain[None, :]

    qt_p, kf_p, kb_p, vf_p, vt_p, u_p, g_p = _project(
        x_prompt[0], _rope_tables(*_cos_sin_range(N_META, seq)), norm_gain, w_n, w_t, PROJ_ROWS)
    n_pad = SMALL_ROWS - N_META - n_samp
    pos_small = jnp.concatenate([jnp.arange(N_META), jnp.tile(p_len + jnp.arange(n_new), n_seq),
                                 jnp.zeros((n_pad,), jnp.int32)])
    x_small = jnp.concatenate([meta_tokens, x_sample.reshape(n_samp, D_MODEL),
                               jnp.zeros((n_pad, D_MODEL), F32)], axis=0)
    qt_s, kf_s, kb_s, vf_s, vt_s, u_s, g_s = _project(
        x_small, _rope_tables(*_cos_sin(pos_small)), norm_gain, w_n, w_t, SMALL_ROWS)
    samp = slice(N_META, N_META + n_samp)

    km = jnp.pad(kb_s[:, :N_META], ((0, 0), (0, LANES - N_META), (0, 0)))
    vtm = jnp.pad(vt_s[:, :, :N_META], ((0, 0), (0, 0), (0, LANES - N_META)))
    attn_p = _prompt_attention(qt_p, kb_p, vt_p, km, vtm, lams, subln_gain.reshape(V_DIM, 1))
    y_prompt = _prompt_merge(attn_p, u_p, u_s[:N_META], g_p, x_prompt[0], w_pool_b, pool_scale,
                             w_out_b, final_gain2)

    q_seq = qt_s[:, :, :, samp].reshape(N_HEADS, 2, V_DIM, n_seq, n_new)
    q_seq = q_seq.transpose(3, 0, 1, 4, 2).reshape(n_seq, N_HEADS * 2 * n_new, V_DIM)
    cache_rows = slice(N_META * N_HEADS, (N_META + n_samp) * N_HEADS)
    pad_new = ((0, 0), (0, LANES - n_new * N_HEADS), (0, 0))
    k_new = jnp.pad(kf_s[cache_rows].astype(BF16).reshape(n_seq, n_new * N_HEADS, V_DIM), pad_new)
    v_new = jnp.pad(vf_s[cache_rows].astype(BF16).reshape(n_seq, n_new * N_HEADS, V_DIM), pad_new)
    n_pool_pages = cache_k.shape[1]
    attn_s = _decode_attention(page_table, q_seq, k_new, v_new,
                               cache_k.reshape(n_pool_pages, PAGE_SIZE * N_HEADS, V_DIM),
                               cache_v.reshape(n_pool_pages, PAGE_SIZE * N_HEADS, V_DIM), lams, subln_gain)
    u_new = u_s[samp].reshape(n_seq, n_new, D_POOL)
    u_ext = jnp.concatenate([state_pool[0], u_new], axis=1)
    u_shifted = jnp.stack([u_ext[:, POOL_STATE - back:POOL_STATE - back + n_new].reshape(-1, D_POOL)
                           for back in range(max(POOL_WINDOWS))])
    y_sample = _sample_merge(attn_s.reshape(n_samp, D_ATTN), u_shifted, g_s[samp],
                             x_sample.reshape(n_samp, D_MODEL), w_pool_b, pool_scale,
                             w_out_b, final_gain2)

    l_p = seq + N_META
    n_meta_rows = N_META * N_HEADS
    new_k_prompt = jnp.concatenate([kf_s[:n_meta_rows], kf_p]).reshape(1, 1, l_p, N_HEADS, V_DIM)
    new_v_prompt = jnp.concatenate([vf_s[:n_meta_rows], vf_p]).reshape(1, 1, l_p, N_HEADS, V_DIM)
    new_pool_prompt = u_p[-POOL_STATE:].reshape(1, 1, POOL_STATE, D_POOL)
    new_k_sample = kf_s[cache_rows].reshape(1, n_seq, n_new, N_HEADS, V_DIM)
    new_v_sample = vf_s[cache_rows].reshape(1, n_seq, n_new, N_HEADS, V_DIM)
    new_pool_sample = u_ext[:, -POOL_STATE:][None]
    return (y_prompt[None], y_sample.reshape(n_seq, n_new, D_MODEL), new_k_prompt, new_v_prompt,
            new_pool_prompt, new_k_sample, new_v_sample, new_pool_sample)
```

```python
import math

import jax
import jax.numpy as jnp
from jax import lax
from jax.experimental import pallas as pl
from jax.experimental.pallas import tpu as pltpu

F32 = jnp.float32
BF16 = jnp.bfloat16

D_MODEL = 1024
N_META = 16
D_ATTN = 512
D_POOL = 512
HEAD_DIM = 64
N_HEADS = 4
V_DIM = 128
POOL_WINDOWS = (2, 4, 8, 16)
POOL_GROUP = 128
POOL_STATE = 15
D_IN = 3 * D_ATTN + D_POOL + D_MODEL
ROPE_THETA = 10000.0
RMS_EPS = 1e-6
SUBLN_EPS = 1e-5
PAGE_SIZE = 128
LANES = 128
HALO = 16
NEG_BIG = -1e30
Q_SCALE = HEAD_DIM ** -0.5 * math.log2(math.e)

PROJ_ROWS = 256
SMALL_ROWS = 256
ATTN_T = 512
ATTN_GROUP = 4
SUM_ROWS = 16
VT_ROWS = V_DIM + SUM_ROWS
MERGE_ROWS = 512
PAGES_PER_STEP = 16
PAGE_GROUP = 4
VMEM_LIMIT = 48 * 1024 * 1024


def _dot(a, b):
    return jnp.dot(a, b, preferred_element_type=F32)


def _dot_nt(a, b):
    return lax.dot_general(a, b, (((1,), (1,)), ((), ())), preferred_element_type=F32)


def _lam_init(layer):
    return 0.8 - 0.6 * math.exp(-0.3 * layer)


def _lam(lq1, lk1, lq2, lk2):
    a = jnp.sum(lq1 * lk1, axis=1, keepdims=True)
    b = jnp.sum(lq2 * lk2, axis=1, keepdims=True)
    return jnp.exp(a) - jnp.exp(b) + _lam_init(0)


def _rms(x, gain, eps):
    return x * lax.rsqrt(jnp.mean(x * x, axis=-1, keepdims=True) + eps) * gain


def _project_kernel(x_ref, cos_ref, sin_ref, cost_ref, sint_ref, gain_ref, wn_ref, wt_ref,
                    qt_ref, kf_ref, kb_ref, vf_ref, vt_ref, u_ref, g_ref):
    rows = x_ref.shape[0]
    half = HEAD_DIM // 2
    xn = _rms(x_ref[...], gain_ref[...], RMS_EPS).astype(BF16)
    z = _dot(xn, wn_ref[...])
    zt = _dot_nt(wt_ref[...], xn)
    cos = cos_ref[...]
    sin = sin_ref[...]
    cost = cost_ref[...]
    sint = sint_ref[...]
    lane = lax.broadcasted_iota(jnp.int32, cos.shape, 1)
    first_half = (lane % HEAD_DIM) < half
    zeros = jnp.zeros((HEAD_DIM, rows), F32)

    for h in range(N_HEADS):
        for c in range(2):
            base = h * V_DIM + c * HEAD_DIM
            x1 = zt[base:base + half]
            x2 = zt[base + half:base + HEAD_DIM]
            o1 = (x1 * cost - x2 * sint) * Q_SCALE
            o2 = (x2 * cost + x1 * sint) * Q_SCALE
            parts = [o1, o2, zeros] if c == 0 else [zeros, o1, o2]
            qt_ref[h, c] = jnp.concatenate(parts, axis=0).astype(BF16)
        vt_ref[h, :V_DIM] = zt[D_ATTN + h * V_DIM:D_ATTN + (h + 1) * V_DIM].astype(BF16)
        vt_ref[h, V_DIM:] = jnp.ones((SUM_ROWS, rows), BF16)

        ks = z[:, h * V_DIM:(h + 1) * V_DIM]
        swapped = jnp.where(first_half, pltpu.roll(ks, LANES - half, axis=1), pltpu.roll(ks, half, axis=1))
        k = ks * cos + swapped * sin
        kf_ref[pl.ds(h, rows, stride=N_HEADS), :] = k
        kb_ref[h] = k.astype(BF16)
        vf_ref[pl.ds(h, rows, stride=N_HEADS), :] = z[:, D_ATTN + h * V_DIM:D_ATTN + (h + 1) * V_DIM]
    u_ref[...] = z[:, 2 * D_ATTN:2 * D_ATTN + D_POOL]
    g_ref[...] = z[:, 2 * D_ATTN + D_POOL:]


def _project(x, tables, gain, w_n, w_t, block_rows):
    rows = x.shape[0]
    assert rows % block_rows == 0
    row_spec = lambda cols: pl.BlockSpec((block_rows, cols), lambda i: (i, 0))
    col_spec = pl.BlockSpec((HEAD_DIM // 2, block_rows), lambda i: (0, i))
    full = lambda *shape: pl.BlockSpec(shape, lambda i: (0,) * len(shape))
    cache_spec = pl.BlockSpec((block_rows * N_HEADS, V_DIM), lambda i: (i, 0))
    cache_shape = jax.ShapeDtypeStruct((rows * N_HEADS, V_DIM), F32)
    return pl.pallas_call(
        _project_kernel,
        grid=(rows // block_rows,),
        in_specs=[row_spec(D_MODEL), row_spec(LANES), row_spec(LANES), col_spec, col_spec,
                  full(1, D_MODEL), full(*w_n.shape), full(*w_t.shape)],
        out_specs=[pl.BlockSpec((N_HEADS, 2, V_DIM, block_rows), lambda i: (0, 0, 0, i)),
                   cache_spec,
                   pl.BlockSpec((N_HEADS, block_rows, V_DIM), lambda i: (0, i, 0)),
                   cache_spec,
                   pl.BlockSpec((N_HEADS, VT_ROWS, block_rows), lambda i: (0, 0, i)),
                   row_spec(D_POOL), row_spec(D_MODEL)],
        out_shape=[jax.ShapeDtypeStruct((N_HEADS, 2, V_DIM, rows), BF16),
                   cache_shape,
                   jax.ShapeDtypeStruct((N_HEADS, rows, V_DIM), BF16),
                   cache_shape,
                   jax.ShapeDtypeStruct((N_HEADS, VT_ROWS, rows), BF16),
                   jax.ShapeDtypeStruct((rows, D_POOL), F32),
                   jax.ShapeDtypeStruct((rows, D_MODEL), F32)],
        compiler_params=pltpu.CompilerParams(
            dimension_semantics=("arbitrary",), vmem_limit_bytes=VMEM_LIMIT),
        name="project",
    )(x, *tables, gain, w_n, w_t)


def _cos_sin(pos):
    inv_freq = ROPE_THETA ** (-jnp.arange(0, HEAD_DIM, 2, dtype=F32) / HEAD_DIM)
    ang = pos.astype(F32)[:, None] * inv_freq[None, :]
    return jnp.cos(ang), jnp.sin(ang)


def _rope_sign():
    half = HEAD_DIM // 2
    return jnp.tile(jnp.concatenate([-jnp.ones((half,), F32), jnp.ones((half,), F32)]), LANES // HEAD_DIM)


def _rope_tables(pos):
    cos, sin = _cos_sin(pos)
    reps = 2 * LANES // HEAD_DIM
    return jnp.tile(cos, (1, reps)), jnp.tile(sin, (1, reps)) * _rope_sign(), cos.T, sin.T


def _rope_tables_range(start, n, block=256):
    assert n % block == 0
    reps = 2 * LANES // HEAD_DIM
    cos_a, sin_a = _cos_sin(start + block * jnp.arange(n // block))
    cos_b, sin_b = _cos_sin(jnp.arange(block))
    ca, sa = jnp.tile(cos_a, (1, reps))[:, None, :], jnp.tile(sin_a, (1, reps))[:, None, :]
    cb, sb = jnp.tile(cos_b, (1, reps))[None], jnp.tile(sin_b, (1, reps))[None]
    cos = (ca * cb - sa * sb).reshape(n, LANES)
    sin = ((sa * cb + ca * sb) * _rope_sign()).reshape(n, LANES)
    cat, sat = cos_a.T[:, :, None], sin_a.T[:, :, None]
    cbt, sbt = cos_b.T[:, None, :], sin_b.T[:, None, :]
    return cos, sin, (cat * cbt - sat * sbt).reshape(-1, n), (sat * cbt + cat * sbt).reshape(-1, n)


def _prompt_attn_kernel(qt_ref, k_ref, vt_ref, km_ref, vtm_ref,
                        lq1_ref, lk1_ref, lq2_ref, lk2_ref, sg_ref, o_ref,
                        m_ref, acc_ref, sa_ref, sb_ref):
    i = pl.program_id(1)
    t = ATTN_T
    qt = jnp.concatenate([qt_ref[0, 0], qt_ref[0, 1]], axis=1)

    def scores(c):
        return _dot(k_ref[0, pl.ds(pl.multiple_of(c * t, t), t), :], qt)

    def values(c):
        return vt_ref[0, :, pl.ds(pl.multiple_of(c * t, t), t)]

    def update(s, vt):
        m_prev = m_ref[...]
        m_new = jnp.maximum(m_prev, jnp.max(s, axis=0, keepdims=True))
        alpha = jnp.exp2(m_prev - m_new)
        p = jnp.exp2(s - m_new).astype(BF16)
        acc_ref[...] = alpha * acc_ref[...] + _dot(vt, p)
        m_ref[...] = m_new

    m_ref[...] = jnp.full(m_ref.shape, NEG_BIG, F32)
    acc_ref[...] = jnp.zeros(acc_ref.shape, F32)

    sa_ref[...] = scores(0)

    key = lax.broadcasted_iota(jnp.int32, (LANES, 2 * t), 0)
    update(jnp.where(key < N_META, _dot(km_ref[0], qt), NEG_BIG), vtm_ref[0])

    def run(first, count, prefetch=None, diagonal_last=False):
        bufs = (sa_ref, sb_ref)
        for n in range(count):
            cur, nxt = bufs[n % 2], bufs[(n + 1) % 2]
            if n + 1 < count:
                nxt[...] = scores(first + n + 1)
            elif prefetch is not None:
                nxt[...] = scores(prefetch)
            s = cur[...]
            if diagonal_last and n == count - 1:
                key = lax.broadcasted_iota(jnp.int32, (t, 2 * t), 0)
                qry = lax.broadcasted_iota(jnp.int32, (t, 2 * t), 1) % t
                s = jnp.where(key <= qry, s, NEG_BIG)
            update(s, values(first + n))

    group = ATTN_GROUP

    def body(n, carry):
        run(group * n, group, prefetch=group * n + group)
        return carry

    lax.fori_loop(0, i // group, body, 0)

    for rest in range(group):
        @pl.when(i % group == rest)
        def _():
            run(i - rest, rest + 1, diagonal_last=True)

    lam = _lam(lq1_ref[...], lk1_ref[...], lq2_ref[...], lk2_ref[...])
    o = acc_ref[:V_DIM, :] / acc_ref[V_DIM:V_DIM + 1, :]
    o = o[:, :t] - lam * o[:, t:]
    o = o * lax.rsqrt(jnp.mean(o * o, axis=0, keepdims=True) + SUBLN_EPS) * sg_ref[...]
    o_ref[...] = (o * (1.0 - _lam_init(0))).T


def _prompt_attention(qt, kb, vt, km, vtm, lams, subln_gain_col):
    rows = kb.shape[1]
    t = ATTN_T
    assert rows % t == 0
    vec = lambda n: pl.BlockSpec((1, n), lambda h, i: (0, 0))
    head = lambda *shape: pl.BlockSpec((1,) + shape, lambda h, i: (h,) + (0,) * len(shape))
    return pl.pallas_call(
        _prompt_attn_kernel,
        grid=(N_HEADS, rows // t),
        in_specs=[pl.BlockSpec((1, 2, V_DIM, t), lambda h, i: (h, 0, 0, i)),
                  head(rows, V_DIM), head(VT_ROWS, rows), head(LANES, V_DIM), head(VT_ROWS, LANES),
                  vec(HEAD_DIM), vec(HEAD_DIM), vec(HEAD_DIM), vec(HEAD_DIM),
                  pl.BlockSpec((V_DIM, 1), lambda h, i: (0, 0))],
        out_specs=pl.BlockSpec((t, V_DIM), lambda h, i: (i, h)),
        out_shape=jax.ShapeDtypeStruct((rows, D_ATTN), F32),
        scratch_shapes=[pltpu.VMEM((1, 2 * t), F32),
                        pltpu.VMEM((VT_ROWS, 2 * t), F32),
                        pltpu.VMEM((t, 2 * t), F32),
                        pltpu.VMEM((t, 2 * t), F32)],
        compiler_params=pltpu.CompilerParams(
            dimension_semantics=("arbitrary", "arbitrary"), vmem_limit_bytes=VMEM_LIMIT),
        name="prompt_attention",
    )(qt, kb, vt, km, vtm, *lams, subln_gain_col)


def _decode_attn_kernel(pt_ref, q_ref, kn_ref, vn_ref, *rest):
    n = PAGES_PER_STEP
    k_refs = rest[:n]
    v_refs = rest[n:2 * n]
    lq1_ref, lk1_ref, lq2_ref, lk2_ref, sg_ref, o_ref, m_ref, l_ref, acc_ref = rest[2 * n:]
    step = pl.program_id(1)
    q = q_ref[0]
    n_rows = q.shape[0]
    n_new = n_rows // (2 * N_HEADS)

    def own_head(n_keys):
        row = lax.broadcasted_iota(jnp.int32, (n_rows, n_keys), 0)
        col = lax.broadcasted_iota(jnp.int32, (n_rows, n_keys), 1)
        return row, col, (col % N_HEADS) == (row // (2 * n_new))

    @pl.when(step == 0)
    def _():
        row, col, own = own_head(kn_ref.shape[1])
        s = jnp.where(own & ((col // N_HEADS) <= (row % n_new)), _dot_nt(q, kn_ref[0]), NEG_BIG)
        m = jnp.max(s, axis=1, keepdims=True)
        p = jnp.exp2(s - m)
        m_ref[...] = jnp.broadcast_to(m, m_ref.shape)
        l_ref[...] = jnp.broadcast_to(jnp.sum(p, axis=1, keepdims=True), l_ref.shape)
        acc_ref[...] = _dot(p.astype(BF16), vn_ref[0])

    _, _, own = own_head(PAGE_SIZE * N_HEADS)
    scores = [jnp.where(own, _dot_nt(q, k_refs[r][0].astype(BF16)), NEG_BIG) for r in range(n)]
    m_prev = m_ref[...]
    l = l_ref[...]
    acc = acc_ref[...]
    for g in range(0, n, PAGE_GROUP):
        m_cur = scores[g]
        for s in scores[g + 1:g + PAGE_GROUP]:
            m_cur = jnp.maximum(m_cur, s)
        m_new = jnp.maximum(m_prev, jnp.max(m_cur, axis=1, keepdims=True))
        alpha = jnp.exp2(m_prev - m_new)
        l = alpha * l
        acc = alpha * acc
        for r in range(g, g + PAGE_GROUP):
            p = jnp.exp2(scores[r] - m_new[:, :1])
            l = l + jnp.sum(p, axis=1, keepdims=True)
            acc = acc + _dot(p.astype(BF16), v_refs[r][0].astype(BF16))
        m_prev = m_new
    m_ref[...] = m_prev
    l_ref[...] = l
    acc_ref[...] = acc

    @pl.when(step == pl.num_programs(1) - 1)
    def _():
        lam = _lam(lq1_ref[...], lk1_ref[...], lq2_ref[...], lk2_ref[...])
        o = acc_ref[...] / l_ref[...]
        o = o - lam * pltpu.roll(o, n_rows - n_new, axis=0)
        o = _rms(o, sg_ref[...], SUBLN_EPS) * (1.0 - _lam_init(0))
        for h in range(N_HEADS):
            o_ref[0, :, h * V_DIM:(h + 1) * V_DIM] = o[h * 2 * n_new:h * 2 * n_new + n_new]


def _decode_attention(page_table, q_seq, k_new, v_new, cache_k, cache_v, lams, subln_gain):
    n_seq, n_pages = page_table.shape
    n_rows = q_seq.shape[1]
    n_new = n_rows // (2 * N_HEADS)
    page_rows = PAGE_SIZE * N_HEADS
    assert n_pages % PAGES_PER_STEP == 0 and cache_k.shape[1:] == (page_rows, V_DIM)
    page_spec = lambda r: pl.BlockSpec(
        (1, page_rows, V_DIM), lambda b, s, pt: (pt[b, s * PAGES_PER_STEP + r], 0, 0))
    seq_spec = lambda *shape: pl.BlockSpec((1,) + shape, lambda b, s, pt: (b,) + (0,) * len(shape))
    vec = lambda n: pl.BlockSpec((1, n), lambda b, s, pt: (0, 0))
    grid_spec = pltpu.PrefetchScalarGridSpec(
        num_scalar_prefetch=1,
        grid=(n_seq, n_pages // PAGES_PER_STEP),
        in_specs=[seq_spec(n_rows, V_DIM), seq_spec(*k_new.shape[1:]), seq_spec(*v_new.shape[1:])]
                 + [page_spec(r) for r in range(PAGES_PER_STEP)] * 2
                 + [vec(HEAD_DIM)] * 4 + [vec(V_DIM)],
        out_specs=seq_spec(n_new, D_ATTN),
        scratch_shapes=[pltpu.VMEM((n_rows, LANES), F32),
                        pltpu.VMEM((n_rows, LANES), F32),
                        pltpu.VMEM((n_rows, V_DIM), F32)],
    )
    return pl.pallas_call(
        _decode_attn_kernel,
        grid_spec=grid_spec,
        out_shape=jax.ShapeDtypeStruct((n_seq, n_new, D_ATTN), F32),
        compiler_params=pltpu.CompilerParams(
            dimension_semantics=("arbitrary", "arbitrary"), vmem_limit_bytes=VMEM_LIMIT),
        name="decode_attention",
    )(page_table, q_seq, k_new, v_new, *([cache_k] * PAGES_PER_STEP), *([cache_v] * PAGES_PER_STEP),
      *lams, subln_gain)


def _merge_tail(attn, pool_diffs, gate, x, wp_ref, ps_ref, wo_ref, fg_ref):
    pool = jnp.concatenate(
        [_dot(pool_diffs[g].astype(BF16), wp_ref[g]) for g in range(len(POOL_WINDOWS))], axis=1)
    mixed = jnp.concatenate([attn, pool * ps_ref[...]], axis=1)
    mixed = mixed * (gate * jax.nn.sigmoid(gate))
    h = x + _dot(mixed.astype(BF16), wo_ref[...])
    return _rms(h, fg_ref[...], RMS_EPS)


def _prompt_merge_kernel(attn_ref, u_ref, um_ref, gate_ref, x_ref, wp_ref, ps_ref, wo_ref, fg_ref,
                         y_ref, ubuf):
    rows = u_ref.shape[0]

    @pl.when(pl.program_id(0) == 0)
    def _():
        ubuf[0:HALO, :] = um_ref[...]

    ubuf[HALO:HALO + rows, :] = u_ref[...]
    diffs = []
    for g, w in enumerate(POOL_WINDOWS):
        sl = slice(g * POOL_GROUP, (g + 1) * POOL_GROUP)
        total = ubuf[HALO:HALO + rows, sl]
        for back in range(1, w):
            total = total + ubuf[HALO - back:HALO - back + rows, sl]
        diffs.append(total / float(w) - ubuf[HALO:HALO + rows, sl])
    y_ref[...] = _merge_tail(attn_ref[...], diffs, gate_ref[...], x_ref[...],
                             wp_ref, ps_ref, wo_ref, fg_ref)
    ubuf[0:HALO, :] = ubuf[rows:rows + HALO, :]


def _prompt_merge(attn, u, u_meta, gate, x, wp_b, pool_scale, wo_b, final_gain):
    rows = x.shape[0]
    t = MERGE_ROWS
    assert rows % t == 0 and u_meta.shape[0] == HALO
    row_spec = lambda cols: pl.BlockSpec((t, cols), lambda i: (i, 0))
    full = lambda *shape: pl.BlockSpec(shape, lambda i: (0,) * len(shape))
    return pl.pallas_call(
        _prompt_merge_kernel,
        grid=(rows // t,),
        in_specs=[row_spec(D_ATTN), row_spec(D_POOL), full(HALO, D_POOL), row_spec(D_MODEL),
                  row_spec(D_MODEL), full(len(POOL_WINDOWS), POOL_GROUP, POOL_GROUP),
                  full(1, D_POOL), full(D_MODEL, D_MODEL), full(1, D_MODEL)],
        out_specs=row_spec(D_MODEL),
        out_shape=jax.ShapeDtypeStruct((rows, D_MODEL), F32),
        scratch_shapes=[pltpu.VMEM((t + HALO, D_POOL), F32)],
        compiler_params=pltpu.CompilerParams(
            dimension_semantics=("arbitrary",), vmem_limit_bytes=VMEM_LIMIT),
        name="prompt_merge",
    )(attn, u, u_meta, gate, x, wp_b, pool_scale, wo_b, final_gain)


def _sample_merge_kernel(attn_ref, st_ref, un_ref, gate_ref, x_ref, wp_ref, ps_ref, wo_ref, fg_ref, y_ref):
    n_state, n_seq, _ = st_ref.shape
    n_new = un_ref.shape[0] // n_seq

    def ext(t, sl):
        if t < n_state:
            return st_ref[t, :, sl]
        return un_ref[(t - n_state) * n_seq:(t - n_state + 1) * n_seq, sl]

    diffs = []
    for g, w in enumerate(POOL_WINDOWS):
        sl = slice(g * POOL_GROUP, (g + 1) * POOL_GROUP)
        per_token = []
        for j in range(n_new):
            total = ext(n_state + j, sl)
            for back in range(1, w):
                total = total + ext(n_state + j - back, sl)
            per_token.append(total / float(w) - ext(n_state + j, sl))
        diffs.append(jnp.concatenate(per_token, axis=0))
    y_ref[...] = _merge_tail(attn_ref[...], diffs, gate_ref[...], x_ref[...],
                             wp_ref, ps_ref, wo_ref, fg_ref)


def _sample_merge(attn, state_t, u_new, gate, x, wp_b, pool_scale, wo_b, final_gain):
    rows = x.shape[0]
    return pl.pallas_call(
        _sample_merge_kernel,
        out_shape=jax.ShapeDtypeStruct((rows, D_MODEL), F32),
        compiler_params=pltpu.CompilerParams(vmem_limit_bytes=VMEM_LIMIT),
        name="sample_merge",
    )(attn, state_t, u_new, gate, x, wp_b, pool_scale, wo_b, final_gain)


def kernel(x_prompt, x_sample, cache_k, cache_v, state_pool, page_table, meta_tokens, norm_gain, w_in,
           lambda_q1, lambda_k1, lambda_q2, lambda_k2, subln_gain, w_pool, pool_scale, w_out, final_gain):
    batch, seq, _ = x_prompt.shape
    n_seq, n_new, _ = x_sample.shape
    n_samp = n_seq * n_new
    assert batch == 1 and cache_k.shape[0] == 1 and N_META + n_samp <= SMALL_ROWS
    p_len = page_table.shape[1] * PAGE_SIZE
    assert N_META + 1 >= max(POOL_WINDOWS) and p_len - POOL_STATE + 1 >= max(POOL_WINDOWS)

    w_n = w_in[0, :, D_ATTN:].astype(BF16)
    w_t = jnp.concatenate([w_in[0, :, :D_ATTN], w_in[0, :, 2 * D_ATTN:3 * D_ATTN]], axis=1).T.astype(BF16)
    w_out_b = w_out[0].astype(BF16)
    w_pool_b = w_pool[0].astype(BF16)
    lams = (lambda_q1, lambda_k1, lambda_q2, lambda_k2)
    final_gain2 = final_gain[None, :]

    n_pad = SMALL_ROWS - N_META - n_samp
    pos_small = jnp.concatenate([jnp.arange(N_META), jnp.repeat(p_len + jnp.arange(n_new), n_seq),
                                 jnp.zeros((n_pad,), jnp.int32)])
    x_samp = x_sample.transpose(1, 0, 2).reshape(n_samp, D_MODEL)
    x_small = jnp.concatenate([meta_tokens, x_samp, jnp.zeros((n_pad, D_MODEL), F32)], axis=0)
    qt_s, kf_s, kb_s, vf_s, vt_s, u_s, g_s = _project(
        x_small, _rope_tables(pos_small), norm_gain, w_n, w_t, SMALL_ROWS)
    samp = slice(N_META, N_META + n_samp)
    n_meta_rows = N_META * N_HEADS
    cache_rows = slice(n_meta_rows, (N_META + n_samp) * N_HEADS)

    qt_p, kf_p, kb_p, vf_p, vt_p, u_p, g_p = _project(
        x_prompt[0], _rope_tables_range(N_META, seq), norm_gain, w_n, w_t, PROJ_ROWS)

    km = jnp.pad(kb_s[:, :N_META], ((0, 0), (0, LANES - N_META), (0, 0)))
    vtm = jnp.pad(vt_s[:, :, :N_META], ((0, 0), (0, 0), (0, LANES - N_META)))
    attn_p = _prompt_attention(qt_p, kb_p, vt_p, km, vtm, lams, subln_gain.reshape(V_DIM, 1))
    y_prompt = _prompt_merge(attn_p, u_p, u_s[:N_META], g_p, x_prompt[0], w_pool_b, pool_scale,
                             w_out_b, final_gain2)

    q_seq = qt_s[:, :, :, samp].reshape(N_HEADS, 2, V_DIM, n_new, n_seq)
    q_seq = q_seq.transpose(4, 0, 1, 3, 2).reshape(n_seq, N_HEADS * 2 * n_new, V_DIM)
    by_seq = lambda a: a.reshape(n_new, n_seq, -1).transpose(1, 0, 2)
    pad_new = ((0, 0), (0, LANES - n_new * N_HEADS), (0, 0))
    k_samp = by_seq(kf_s[cache_rows])
    v_samp = by_seq(vf_s[cache_rows])
    k_new = jnp.pad(k_samp.astype(BF16).reshape(n_seq, n_new * N_HEADS, V_DIM), pad_new)
    v_new = jnp.pad(v_samp.astype(BF16).reshape(n_seq, n_new * N_HEADS, V_DIM), pad_new)
    n_pool_pages = cache_k.shape[1]
    attn_s = _decode_attention(page_table, q_seq, k_new, v_new,
                               cache_k.reshape(n_pool_pages, PAGE_SIZE * N_HEADS, V_DIM),
                               cache_v.reshape(n_pool_pages, PAGE_SIZE * N_HEADS, V_DIM), lams, subln_gain)
    y_samp = _sample_merge(attn_s.transpose(1, 0, 2).reshape(n_samp, D_ATTN),
                           state_pool[0].transpose(1, 0, 2), u_s[samp], g_s[samp], x_samp,
                           w_pool_b, pool_scale, w_out_b, final_gain2)

    l_p = seq + N_META
    new_k_prompt = jnp.concatenate([kf_s[:n_meta_rows], kf_p]).reshape(1, 1, l_p, N_HEADS, V_DIM)
    new_v_prompt = jnp.concatenate([vf_s[:n_meta_rows], vf_p]).reshape(1, 1, l_p, N_HEADS, V_DIM)
    new_pool_prompt = u_p[-POOL_STATE:].reshape(1, 1, POOL_STATE, D_POOL)
    new_k_sample = k_samp.reshape(1, n_seq, n_new, N_HEADS, V_DIM)
    new_v_sample = v_samp.reshape(1, n_seq, n_new, N_HEADS, V_DIM)
    new_pool_sample = jnp.concatenate([state_pool[0], by_seq(u_s[samp])], axis=1)[None, :, -POOL_STATE:]
    return (y_prompt[None], by_seq(y_samp), new_k_prompt, new_v_prompt,
            new_pool_prompt, new_k_sample, new_v_sample, new_pool_sample)
```

```python
import math

import jax
import jax.numpy as jnp
from jax import lax
from jax.experimental import pallas as pl
from jax.experimental.pallas import tpu as pltpu

F32 = jnp.float32
BF16 = jnp.bfloat16

D_MODEL = 1024
N_META = 16
D_ATTN = 512
D_POOL = 512
HEAD_DIM = 64
N_HEADS = 4
V_DIM = 128
POOL_WINDOWS = (2, 4, 8, 16)
POOL_GROUP = 128
POOL_STATE = 15
D_IN = 3 * D_ATTN + D_POOL + D_MODEL
ROPE_THETA = 10000.0
RMS_EPS = 1e-6
SUBLN_EPS = 1e-5
PAGE_SIZE = 128
LANES = 128
HALO = 16
NEG_BIG = -1e30
Q_SCALE = HEAD_DIM ** -0.5 * math.log2(math.e)

PROJ_ROWS = 256
SMALL_ROWS = 256
ATTN_T = 512
ATTN_GROUP = 4
SUM_ROWS = 16
VT_ROWS = V_DIM + SUM_ROWS
MERGE_ROWS = 512
PAGES_PER_STEP = 16
PAGE_GROUP = 4
VMEM_LIMIT = 48 * 1024 * 1024


def _dot(a, b):
    return jnp.dot(a, b, preferred_element_type=F32)


def _dot_nt(a, b):
    return lax.dot_general(a, b, (((1,), (1,)), ((), ())), preferred_element_type=F32)


def _lam_init(layer):
    return 0.8 - 0.6 * math.exp(-0.3 * layer)


def _lam(lq1, lk1, lq2, lk2):
    a = jnp.sum(lq1 * lk1, axis=1, keepdims=True)
    b = jnp.sum(lq2 * lk2, axis=1, keepdims=True)
    return jnp.exp(a) - jnp.exp(b) + _lam_init(0)


def _rms(x, gain, eps):
    return x * lax.rsqrt(jnp.mean(x * x, axis=-1, keepdims=True) + eps) * gain


def _project_kernel(x_ref, cos_ref, sin_ref, cost_ref, sint_ref, gain_ref, wn_ref, wt_ref,
                    qt_ref, kf_ref, kb_ref, vf_ref, vt_ref, u_ref, g_ref):
    rows = x_ref.shape[0]
    half = HEAD_DIM // 2
    xn = _rms(x_ref[...], gain_ref[...], RMS_EPS).astype(BF16)
    z = _dot(xn, wn_ref[...])
    zt = _dot_nt(wt_ref[...], xn)
    cos = cos_ref[...]
    sin = sin_ref[...]
    cost = cost_ref[...]
    sint = sint_ref[...]
    lane = lax.broadcasted_iota(jnp.int32, cos.shape, 1)
    first_half = (lane % HEAD_DIM) < half
    zeros = jnp.zeros((HEAD_DIM, rows), F32)

    for h in range(N_HEADS):
        for c in range(2):
            base = h * V_DIM + c * HEAD_DIM
            x1 = zt[base:base + half]
            x2 = zt[base + half:base + HEAD_DIM]
            o1 = (x1 * cost - x2 * sint) * Q_SCALE
            o2 = (x2 * cost + x1 * sint) * Q_SCALE
            parts = [o1, o2, zeros] if c == 0 else [zeros, o1, o2]
            qt_ref[h, c] = jnp.concatenate(parts, axis=0).astype(BF16)
        v = z[:, D_ATTN + h * V_DIM:D_ATTN + (h + 1) * V_DIM]
        vt_ref[h, :V_DIM] = v.T.astype(BF16)
        vt_ref[h, V_DIM:] = jnp.ones((SUM_ROWS, rows), BF16)

        ks = z[:, h * V_DIM:(h + 1) * V_DIM]
        swapped = jnp.where(first_half, pltpu.roll(ks, LANES - half, axis=1), pltpu.roll(ks, half, axis=1))
        k = ks * cos + swapped * sin
        kf_ref[pl.ds(h, rows, stride=N_HEADS), :] = k
        kb_ref[h] = k.astype(BF16)
        vf_ref[pl.ds(h, rows, stride=N_HEADS), :] = z[:, D_ATTN + h * V_DIM:D_ATTN + (h + 1) * V_DIM]
    u_ref[...] = z[:, 2 * D_ATTN:2 * D_ATTN + D_POOL]
    g_ref[...] = z[:, 2 * D_ATTN + D_POOL:]


def _project(x, tables, gain, w_n, w_t, block_rows):
    rows = x.shape[0]
    assert rows % block_rows == 0
    row_spec = lambda cols: pl.BlockSpec((block_rows, cols), lambda i: (i, 0))
    col_spec = pl.BlockSpec((HEAD_DIM // 2, block_rows), lambda i: (0, i))
    full = lambda *shape: pl.BlockSpec(shape, lambda i: (0,) * len(shape))
    cache_spec = pl.BlockSpec((block_rows * N_HEADS, V_DIM), lambda i: (i, 0))
    cache_shape = jax.ShapeDtypeStruct((rows * N_HEADS, V_DIM), F32)
    return pl.pallas_call(
        _project_kernel,
        grid=(rows // block_rows,),
        in_specs=[row_spec(D_MODEL), row_spec(LANES), row_spec(LANES), col_spec, col_spec,
                  full(1, D_MODEL), full(*w_n.shape), full(*w_t.shape)],
        out_specs=[pl.BlockSpec((N_HEADS, 2, V_DIM, block_rows), lambda i: (0, 0, 0, i)),
                   cache_spec,
                   pl.BlockSpec((N_HEADS, block_rows, V_DIM), lambda i: (0, i, 0)),
                   cache_spec,
                   pl.BlockSpec((N_HEADS, VT_ROWS, block_rows), lambda i: (0, 0, i)),
                   row_spec(D_POOL), row_spec(D_MODEL)],
        out_shape=[jax.ShapeDtypeStruct((N_HEADS, 2, V_DIM, rows), BF16),
                   cache_shape,
                   jax.ShapeDtypeStruct((N_HEADS, rows, V_DIM), BF16),
                   cache_shape,
                   jax.ShapeDtypeStruct((N_HEADS, VT_ROWS, rows), BF16),
                   jax.ShapeDtypeStruct((rows, D_POOL), F32),
                   jax.ShapeDtypeStruct((rows, D_MODEL), F32)],
        compiler_params=pltpu.CompilerParams(
            dimension_semantics=("arbitrary",), vmem_limit_bytes=VMEM_LIMIT),
        name="project",
    )(x, *tables, gain, w_n, w_t)


def _cos_sin(pos):
    inv_freq = ROPE_THETA ** (-jnp.arange(0, HEAD_DIM, 2, dtype=F32) / HEAD_DIM)
    ang = pos.astype(F32)[:, None] * inv_freq[None, :]
    return jnp.cos(ang), jnp.sin(ang)


def _rope_sign():
    half = HEAD_DIM // 2
    return jnp.tile(jnp.concatenate([-jnp.ones((half,), F32), jnp.ones((half,), F32)]), LANES // HEAD_DIM)


def _rope_tables(pos):
    cos, sin = _cos_sin(pos)
    reps = 2 * LANES // HEAD_DIM
    return jnp.tile(cos, (1, reps)), jnp.tile(sin, (1, reps)) * _rope_sign(), cos.T, sin.T


def _rope_tables_range(start, n, block=256):
    assert n % block == 0
    reps = 2 * LANES // HEAD_DIM
    cos_a, sin_a = _cos_sin(start + block * jnp.arange(n // block))
    cos_b, sin_b = _cos_sin(jnp.arange(block))
    ca, sa = jnp.tile(cos_a, (1, reps))[:, None, :], jnp.tile(sin_a, (1, reps))[:, None, :]
    cb, sb = jnp.tile(cos_b, (1, reps))[None], jnp.tile(sin_b, (1, reps))[None]
    cos = (ca * cb - sa * sb).reshape(n, LANES)
    sin = ((sa * cb + ca * sb) * _rope_sign()).reshape(n, LANES)
    cat, sat = cos_a.T[:, :, None], sin_a.T[:, :, None]
    cbt, sbt = cos_b.T[:, None, :], sin_b.T[:, None, :]
    return cos, sin, (cat * cbt - sat * sbt).reshape(-1, n), (sat * cbt + cat * sbt).reshape(-1, n)


def _prompt_attn_kernel(qt_ref, k_ref, vt_ref, km_ref, vtm_ref,
                        lq1_ref, lk1_ref, lq2_ref, lk2_ref, sg_ref, o_ref,
                        m_ref, acc_ref, sa_ref, sb_ref):
    i = pl.program_id(1)
    t = ATTN_T
    qt = jnp.concatenate([qt_ref[0, 0], qt_ref[0, 1]], axis=1)

    def scores(c):
        return _dot(k_ref[0, pl.ds(pl.multiple_of(c * t, t), t), :], qt)

    def values(c):
        return vt_ref[0, :, pl.ds(pl.multiple_of(c * t, t), t)]

    def update(s, vt):
        m_prev = m_ref[...]
        m_new = jnp.maximum(m_prev, jnp.max(s, axis=0, keepdims=True))
        alpha = jnp.exp2(m_prev - m_new)
        p = jnp.exp2(s - m_new).astype(BF16)
        acc_ref[...] = alpha * acc_ref[...] + _dot(vt, p)
        m_ref[...] = m_new

    m_ref[...] = jnp.full(m_ref.shape, NEG_BIG, F32)
    acc_ref[...] = jnp.zeros(acc_ref.shape, F32)

    key = lax.broadcasted_iota(jnp.int32, (LANES, 2 * t), 0)
    s_meta = jnp.where(key < N_META, _dot(km_ref[0], qt), NEG_BIG)
    sa_ref[...] = scores(0)
    update(s_meta, vtm_ref[0])

    def run(first, count, prefetch=None, diagonal_last=False):
        bufs = (sa_ref, sb_ref)
        for n in range(count):
            cur, nxt = bufs[n % 2], bufs[(n + 1) % 2]
            if n + 1 < count:
                nxt[...] = scores(first + n + 1)
            elif prefetch is not None:
                nxt[...] = scores(prefetch)
            s = cur[...]
            if diagonal_last and n == count - 1:
                key = lax.broadcasted_iota(jnp.int32, (t, 2 * t), 0)
                qry = lax.broadcasted_iota(jnp.int32, (t, 2 * t), 1) % t
                s = jnp.where(key <= qry, s, NEG_BIG)
            update(s, values(first + n))

    group = ATTN_GROUP

    def body(n, carry):
        run(group * n, group, prefetch=group * n + group)
        return carry

    lax.fori_loop(0, i // group, body, 0)

    for rest in range(group):
        @pl.when(i % group == rest)
        def _():
            run(i - rest, rest + 1, diagonal_last=True)

    lam = _lam(lq1_ref[...], lk1_ref[...], lq2_ref[...], lk2_ref[...])
    o = acc_ref[:V_DIM, :] / acc_ref[V_DIM:V_DIM + 1, :]
    o = o[:, :t] - lam * o[:, t:]
    o = o * lax.rsqrt(jnp.mean(o * o, axis=0, keepdims=True) + SUBLN_EPS) * sg_ref[...]
    o_ref[...] = (o * (1.0 - _lam_init(0))).T


def _prompt_attention(qt, kb, vt, km, vtm, lams, subln_gain_col):
    rows = kb.shape[1]
    t = ATTN_T
    assert rows % t == 0
    vec = lambda n: pl.BlockSpec((1, n), lambda h, i: (0, 0))
    head = lambda *shape: pl.BlockSpec((1,) + shape, lambda h, i: (h,) + (0,) * len(shape))
    return pl.pallas_call(
        _prompt_attn_kernel,
        grid=(N_HEADS, rows // t),
        in_specs=[pl.BlockSpec((1, 2, V_DIM, t), lambda h, i: (h, 0, 0, i)),
                  head(rows, V_DIM), head(VT_ROWS, rows), head(LANES, V_DIM), head(VT_ROWS, LANES),
                  vec(HEAD_DIM), vec(HEAD_DIM), vec(HEAD_DIM), vec(HEAD_DIM),
                  pl.BlockSpec((V_DIM, 1), lambda h, i: (0, 0))],
        out_specs=pl.BlockSpec((t, V_DIM), lambda h, i: (i, h)),
        out_shape=jax.ShapeDtypeStruct((rows, D_ATTN), F32),
        scratch_shapes=[pltpu.VMEM((1, 2 * t), F32),
                        pltpu.VMEM((VT_ROWS, 2 * t), F32),
                        pltpu.VMEM((t, 2 * t), F32),
                        pltpu.VMEM((t, 2 * t), F32)],
        compiler_params=pltpu.CompilerParams(
            dimension_semantics=("arbitrary", "arbitrary"), vmem_limit_bytes=VMEM_LIMIT),
        name="prompt_attention",
    )(qt, kb, vt, km, vtm, *lams, subln_gain_col)


def _decode_attn_kernel(pt_ref, q_ref, kn_ref, vn_ref, *rest):
    n = PAGES_PER_STEP
    k_refs = rest[:n]
    v_refs = rest[n:2 * n]
    lq1_ref, lk1_ref, lq2_ref, lk2_ref, sg_ref, o_ref, m_ref, l_ref, acc_ref = rest[2 * n:]
    step = pl.program_id(1)
    q = q_ref[0]
    n_rows = q.shape[0]
    n_new = n_rows // (2 * N_HEADS)

    def own_head(n_keys):
        row = lax.broadcasted_iota(jnp.int32, (n_rows, n_keys), 0)
        col = lax.broadcasted_iota(jnp.int32, (n_rows, n_keys), 1)
        return row, col, (col % N_HEADS) == (row // (2 * n_new))

    @pl.when(step == 0)
    def _():
        row, col, own = own_head(kn_ref.shape[1])
        s = jnp.where(own & ((col // N_HEADS) <= (row % n_new)), _dot_nt(q, kn_ref[0]), NEG_BIG)
        m = jnp.max(s, axis=1, keepdims=True)
        p = jnp.exp2(s - m)
        m_ref[...] = jnp.broadcast_to(m, m_ref.shape)
        l_ref[...] = jnp.broadcast_to(jnp.sum(p, axis=1, keepdims=True), l_ref.shape)
        acc_ref[...] = _dot(p.astype(BF16), vn_ref[0])

    _, _, own = own_head(PAGE_SIZE * N_HEADS)
    scores = [jnp.where(own, _dot_nt(q, k_refs[r][0].astype(BF16)), NEG_BIG) for r in range(n)]
    m_prev = m_ref[...]
    l = l_ref[...]
    acc = acc_ref[...]
    for g in range(0, n, PAGE_GROUP):
        m_cur = scores[g]
        for s in scores[g + 1:g + PAGE_GROUP]:
            m_cur = jnp.maximum(m_cur, s)
        m_new = jnp.maximum(m_prev, jnp.max(m_cur, axis=1, keepdims=True))
        alpha = jnp.exp2(m_prev - m_new)
        l = alpha * l
        acc = alpha * acc
        for r in range(g, g + PAGE_GROUP):
            p = jnp.exp2(scores[r] - m_new[:, :1])
            l = l + jnp.sum(p, axis=1, keepdims=True)
            acc = acc + _dot(p.astype(BF16), v_refs[r][0].astype(BF16))
        m_prev = m_new
    m_ref[...] = m_prev
    l_ref[...] = l
    acc_ref[...] = acc

    @pl.when(step == pl.num_programs(1) - 1)
    def _():
        lam = _lam(lq1_ref[...], lk1_ref[...], lq2_ref[...], lk2_ref[...])
        o = acc_ref[...] / l_ref[...]
        o = o - lam * pltpu.roll(o, n_rows - n_new, axis=0)
        o = _rms(o, sg_ref[...], SUBLN_EPS) * (1.0 - _lam_init(0))
        for h in range(N_HEADS):
            o_ref[0, :, h * V_DIM:(h + 1) * V_DIM] = o[h * 2 * n_new:h * 2 * n_new + n_new]


def _decode_attention(page_table, q_seq, k_new, v_new, cache_k, cache_v, lams, subln_gain):
    n_seq, n_pages = page_table.shape
    n_rows = q_seq.shape[1]
    n_new = n_rows // (2 * N_HEADS)
    page_rows = PAGE_SIZE * N_HEADS
    assert n_pages % PAGES_PER_STEP == 0 and cache_k.shape[1:] == (page_rows, V_DIM)
    page_spec = lambda r: pl.BlockSpec(
        (1, page_rows, V_DIM), lambda b, s, pt: (pt[b, s * PAGES_PER_STEP + r], 0, 0))
    seq_spec = lambda *shape: pl.BlockSpec((1,) + shape, lambda b, s, pt: (b,) + (0,) * len(shape))
    vec = lambda n: pl.BlockSpec((1, n), lambda b, s, pt: (0, 0))
    grid_spec = pltpu.PrefetchScalarGridSpec(
        num_scalar_prefetch=1,
        grid=(n_seq, n_pages // PAGES_PER_STEP),
        in_specs=[seq_spec(n_rows, V_DIM), seq_spec(*k_new.shape[1:]), seq_spec(*v_new.shape[1:])]
                 + [page_spec(r) for r in range(PAGES_PER_STEP)] * 2
                 + [vec(HEAD_DIM)] * 4 + [vec(V_DIM)],
        out_specs=seq_spec(n_new, D_ATTN),
        scratch_shapes=[pltpu.VMEM((n_rows, LANES), F32),
                        pltpu.VMEM((n_rows, LANES), F32),
                        pltpu.VMEM((n_rows, V_DIM), F32)],
    )
    return pl.pallas_call(
        _decode_attn_kernel,
        grid_spec=grid_spec,
        out_shape=jax.ShapeDtypeStruct((n_seq, n_new, D_ATTN), F32),
        compiler_params=pltpu.CompilerParams(
            dimension_semantics=("arbitrary", "arbitrary"), vmem_limit_bytes=VMEM_LIMIT),
        name="decode_attention",
    )(page_table, q_seq, k_new, v_new, *([cache_k] * PAGES_PER_STEP), *([cache_v] * PAGES_PER_STEP),
      *lams, subln_gain)


def _merge_tail(attn, pool_diffs, gate, x, wp_ref, ps_ref, wo_ref, fg_ref):
    pool = jnp.concatenate(
        [_dot(pool_diffs[g].astype(BF16), wp_ref[g]) for g in range(len(POOL_WINDOWS))], axis=1)
    mixed = jnp.concatenate([attn, pool * ps_ref[...]], axis=1)
    mixed = mixed * (gate * jax.nn.sigmoid(gate))
    h = x + _dot(mixed.astype(BF16), wo_ref[...])
    return _rms(h, fg_ref[...], RMS_EPS)


def _prompt_merge_kernel(attn_ref, u_ref, um_ref, gate_ref, x_ref, wp_ref, ps_ref, wo_ref, fg_ref,
                         y_ref, ubuf):
    rows = u_ref.shape[0]

    @pl.when(pl.program_id(0) == 0)
    def _():
        ubuf[0:HALO, :] = um_ref[...]

    ubuf[HALO:HALO + rows, :] = u_ref[...]
    diffs = []
    for g, w in enumerate(POOL_WINDOWS):
        sl = slice(g * POOL_GROUP, (g + 1) * POOL_GROUP)
        total = ubuf[HALO:HALO + rows, sl]
        for back in range(1, w):
            total = total + ubuf[HALO - back:HALO - back + rows, sl]
        diffs.append(total / float(w) - ubuf[HALO:HALO + rows, sl])
    y_ref[...] = _merge_tail(attn_ref[...], diffs, gate_ref[...], x_ref[...],
                             wp_ref, ps_ref, wo_ref, fg_ref)
    ubuf[0:HALO, :] = ubuf[rows:rows + HALO, :]


def _prompt_merge(attn, u, u_meta, gate, x, wp_b, pool_scale, wo_b, final_gain):
    rows = x.shape[0]
    t = MERGE_ROWS
    assert rows % t == 0 and u_meta.shape[0] == HALO
    row_spec = lambda cols: pl.BlockSpec((t, cols), lambda i: (i, 0))
    full = lambda *shape: pl.BlockSpec(shape, lambda i: (0,) * len(shape))
    return pl.pallas_call(
        _prompt_merge_kernel,
        grid=(rows // t,),
        in_specs=[row_spec(D_ATTN), row_spec(D_POOL), full(HALO, D_POOL), row_spec(D_MODEL),
                  row_spec(D_MODEL), full(len(POOL_WINDOWS), POOL_GROUP, POOL_GROUP),
                  full(1, D_POOL), full(D_MODEL, D_MODEL), full(1, D_MODEL)],
        out_specs=row_spec(D_MODEL),
        out_shape=jax.ShapeDtypeStruct((rows, D_MODEL), F32),
        scratch_shapes=[pltpu.VMEM((t + HALO, D_POOL), F32)],
        compiler_params=pltpu.CompilerParams(
            dimension_semantics=("arbitrary",), vmem_limit_bytes=VMEM_LIMIT),
        name="prompt_merge",
    )(attn, u, u_meta, gate, x, wp_b, pool_scale, wo_b, final_gain)


def _sample_merge_kernel(attn_ref, st_ref, un_ref, gate_ref, x_ref, wp_ref, ps_ref, wo_ref, fg_ref, y_ref):
    n_state, n_seq, _ = st_ref.shape
    n_new = un_ref.shape[0] // n_seq

    def ext(t, sl):
        if t < n_state:
            return st_ref[t, :, sl]
        return un_ref[(t - n_state) * n_seq:(t - n_state + 1) * n_seq, sl]

    diffs = []
    for g, w in enumerate(POOL_WINDOWS):
        sl = slice(g * POOL_GROUP, (g + 1) * POOL_GROUP)
        per_token = []
        for j in range(n_new):
            total = ext(n_state + j, sl)
            for back in range(1, w):
                total = total + ext(n_state + j - back, sl)
            per_token.append(total / float(w) - ext(n_state + j, sl))
        diffs.append(jnp.concatenate(per_token, axis=0))
    y_ref[...] = _merge_tail(attn_ref[...], diffs, gate_ref[...], x_ref[...],
                             wp_ref, ps_ref, wo_ref, fg_ref)


def _sample_merge(attn, state_t, u_new, gate, x, wp_b, pool_scale, wo_b, final_gain):
    rows = x.shape[0]
    return pl.pallas_call(
        _sample_merge_kernel,
        out_shape=jax.ShapeDtypeStruct((rows, D_MODEL), F32),
        compiler_params=pltpu.CompilerParams(vmem_limit_bytes=VMEM_LIMIT),
        name="sample_merge",
    )(attn, state_t, u_new, gate, x, wp_b, pool_scale, wo_b, final_gain)


def kernel(x_prompt, x_sample, cache_k, cache_v, state_pool, page_table, meta_tokens, norm_gain, w_in,
           lambda_q1, lambda_k1, lambda_q2, lambda_k2, subln_gain, w_pool, pool_scale, w_out, final_gain):
    batch, seq, _ = x_prompt.shape
    n_seq, n_new, _ = x_sample.shape
    n_samp = n_seq * n_new
    assert batch == 1 and cache_k.shape[0] == 1 and N_META + n_samp <= SMALL_ROWS
    p_len = page_table.shape[1] * PAGE_SIZE
    assert N_META + 1 >= max(POOL_WINDOWS) and p_len - POOL_STATE + 1 >= max(POOL_WINDOWS)

    w_n = w_in[0, :, D_ATTN:].astype(BF16)
    w_t = w_in[0, :, :D_ATTN].T.astype(BF16)
    w_out_b = w_out[0].astype(BF16)
    w_pool_b = w_pool[0].astype(BF16)
    lams = (lambda_q1, lambda_k1, lambda_q2, lambda_k2)
    final_gain2 = final_gain[None, :]

    n_pad = SMALL_ROWS - N_META - n_samp
    pos_small = jnp.concatenate([jnp.arange(N_META), jnp.repeat(p_len + jnp.arange(n_new), n_seq),
                                 jnp.zeros((n_pad,), jnp.int32)])
    x_samp = x_sample.transpose(1, 0, 2).reshape(n_samp, D_MODEL)
    x_small = jnp.concatenate([meta_tokens, x_samp, jnp.zeros((n_pad, D_MODEL), F32)], axis=0)
    qt_s, kf_s, kb_s, vf_s, vt_s, u_s, g_s = _project(
        x_small, _rope_tables(pos_small), norm_gain, w_n, w_t, SMALL_ROWS)
    samp = slice(N_META, N_META + n_samp)
    n_meta_rows = N_META * N_HEADS
    cache_rows = slice(n_meta_rows, (N_META + n_samp) * N_HEADS)

    qt_p, kf_p, kb_p, vf_p, vt_p, u_p, g_p = _project(
        x_prompt[0], _rope_tables_range(N_META, seq), norm_gain, w_n, w_t, PROJ_ROWS)

    km = jnp.pad(kb_s[:, :N_META], ((0, 0), (0, LANES - N_META), (0, 0)))
    vtm = jnp.pad(vt_s[:, :, :N_META], ((0, 0), (0, 0), (0, LANES - N_META)))
    attn_p = _prompt_attention(qt_p, kb_p, vt_p, km, vtm, lams, subln_gain.reshape(V_DIM, 1))
    y_prompt = _prompt_merge(attn_p, u_p, u_s[:N_META], g_p, x_prompt[0], w_pool_b, pool_scale,
                             w_out_b, final_gain2)

    q_seq = qt_s[:, :, :, samp].reshape(N_HEADS, 2, V_DIM, n_new, n_seq)
    q_seq = q_seq.transpose(4, 0, 1, 3, 2).reshape(n_seq, N_HEADS * 2 * n_new, V_DIM)
    by_seq = lambda a: a.reshape(n_new, n_seq, -1).transpose(1, 0, 2)
    pad_new = ((0, 0), (0, LANES - n_new * N_HEADS), (0, 0))
    k_samp = by_seq(kf_s[cache_rows])
    v_samp = by_seq(vf_s[cache_rows])
    k_new = jnp.pad(k_samp.astype(BF16).reshape(n_seq, n_new * N_HEADS, V_DIM), pad_new)
    v_new = jnp.pad(v_samp.astype(BF16).reshape(n_seq, n_new * N_HEADS, V_DIM), pad_new)
    n_pool_pages = cache_k.shape[1]
    attn_s = _decode_attention(page_table, q_seq, k_new, v_new,
                               cache_k.reshape(n_pool_pages, PAGE_SIZE * N_HEADS, V_DIM),
                               cache_v.reshape(n_pool_pages, PAGE_SIZE * N_HEADS, V_DIM), lams, subln_gain)
    y_samp = _sample_merge(attn_s.transpose(1, 0, 2).reshape(n_samp, D_ATTN),
                           state_pool[0].transpose(1, 0, 2), u_s[samp], g_s[samp], x_samp,
                           w_pool_b, pool_scale, w_out_b, final_gain2)

    l_p = seq + N_META
    new_k_prompt = jnp.concatenate([kf_s[:n_meta_rows], kf_p]).reshape(1, 1, l_p, N_HEADS, V_DIM)
    new_v_prompt = jnp.concatenate([vf_s[:n_meta_rows], vf_p]).reshape(1, 1, l_p, N_HEADS, V_DIM)
    new_pool_prompt = u_p[-POOL_STATE:].reshape(1, 1, POOL_STATE, D_POOL)
    new_k_sample = k_samp.reshape(1, n_seq, n_new, N_HEADS, V_DIM)
    new_v_sample = v_samp.reshape(1, n_seq, n_new, N_HEADS, V_DIM)
    new_pool_sample = jnp.concatenate([state_pool[0], by_seq(u_s[samp])], axis=1)[None, :, -POOL_STATE:]
    return (y_prompt[None], by_seq(y_samp), new_k_prompt, new_v_prompt,
            new_pool_prompt, new_k_sample, new_v_sample, new_pool_sample)
```

```python
import math

import jax
import jax.numpy as jnp
from jax import lax
from jax.experimental import pallas as pl
from jax.experimental.pallas import tpu as pltpu

F32 = jnp.float32
BF16 = jnp.bfloat16

D_MODEL = 1024
N_META = 16
D_ATTN = 512
D_POOL = 512
HEAD_DIM = 64
N_HEADS = 4
V_DIM = 128
POOL_WINDOWS = (2, 4, 8, 16)
POOL_GROUP = 128
POOL_STATE = 15
D_IN = 3 * D_ATTN + D_POOL + D_MODEL
ROPE_THETA = 10000.0
RMS_EPS = 1e-6
SUBLN_EPS = 1e-5
PAGE_SIZE = 128
LANES = 128
HALO = 16
NEG_BIG = -1e30
Q_SCALE = HEAD_DIM ** -0.5 * math.log2(math.e)

PROJ_ROWS = 256
SMALL_ROWS = 256
ATTN_T = 512
ATTN_GROUP = 6
SUM_ROWS = 16
VT_ROWS = V_DIM + SUM_ROWS
MERGE_ROWS = 512
PAGES_PER_STEP = 16
PAGE_GROUP = 4
VMEM_LIMIT = 48 * 1024 * 1024


def _dot(a, b):
    return jnp.dot(a, b, preferred_element_type=F32)


def _dot_nt(a, b):
    return lax.dot_general(a, b, (((1,), (1,)), ((), ())), preferred_element_type=F32)


def _lam_init(layer):
    return 0.8 - 0.6 * math.exp(-0.3 * layer)


def _lam(lq1, lk1, lq2, lk2):
    a = jnp.sum(lq1 * lk1, axis=1, keepdims=True)
    b = jnp.sum(lq2 * lk2, axis=1, keepdims=True)
    return jnp.exp(a) - jnp.exp(b) + _lam_init(0)


def _rms(x, gain, eps):
    return x * lax.rsqrt(jnp.mean(x * x, axis=-1, keepdims=True) + eps) * gain


def _project_kernel(x_ref, cos_ref, sin_ref, cost_ref, sint_ref, gain_ref, wn_ref, wt_ref,
                    qt_ref, kf_ref, kb_ref, vf_ref, vt_ref, u_ref, g_ref):
    rows = x_ref.shape[0]
    half = HEAD_DIM // 2
    xn = _rms(x_ref[...], gain_ref[...], RMS_EPS).astype(BF16)
    z = _dot(xn, wn_ref[...])
    zt = _dot_nt(wt_ref[...], xn)
    cos = cos_ref[...]
    sin = sin_ref[...]
    cost = cost_ref[...]
    sint = sint_ref[...]
    lane = lax.broadcasted_iota(jnp.int32, cos.shape, 1)
    first_half = (lane % HEAD_DIM) < half
    zeros = jnp.zeros((HEAD_DIM, rows), F32)

    for h in range(N_HEADS):
        for c in range(2):
            base = h * V_DIM + c * HEAD_DIM
            x1 = zt[base:base + half]
            x2 = zt[base + half:base + HEAD_DIM]
            o1 = (x1 * cost - x2 * sint) * Q_SCALE
            o2 = (x2 * cost + x1 * sint) * Q_SCALE
            parts = [o1, o2, zeros] if c == 0 else [zeros, o1, o2]
            qt_ref[h, c] = jnp.concatenate(parts, axis=0).astype(BF16)
        v = z[:, D_ATTN + h * V_DIM:D_ATTN + (h + 1) * V_DIM]
        vt_ref[h, :V_DIM] = v.T.astype(BF16)
        vt_ref[h, V_DIM:] = jnp.ones((SUM_ROWS, rows), BF16)

        ks = z[:, h * V_DIM:(h + 1) * V_DIM]
        swapped = jnp.where(first_half, pltpu.roll(ks, LANES - half, axis=1), pltpu.roll(ks, half, axis=1))
        k = ks * cos + swapped * sin
        kf_ref[pl.ds(h, rows, stride=N_HEADS), :] = k
        kb_ref[h] = k.astype(BF16)
        vf_ref[pl.ds(h, rows, stride=N_HEADS), :] = z[:, D_ATTN + h * V_DIM:D_ATTN + (h + 1) * V_DIM]
    u_ref[...] = z[:, 2 * D_ATTN:2 * D_ATTN + D_POOL]
    gate = z[:, 2 * D_ATTN + D_POOL:]
    g_ref[...] = (gate * jax.nn.sigmoid(gate)).astype(BF16)


def _project(x, tables, gain, w_n, w_t, block_rows, lead_rows=0):
    rows = x.shape[0]
    assert rows % block_rows == 0
    row_spec = lambda cols: pl.BlockSpec((block_rows, cols), lambda i: (i, 0))
    col_spec = pl.BlockSpec((HEAD_DIM // 2, block_rows), lambda i: (0, i))
    full = lambda *shape: pl.BlockSpec(shape, lambda i: (0,) * len(shape))
    cache_block = block_rows * N_HEADS
    cache_spec = pl.BlockSpec((pl.Element(cache_block), pl.Element(V_DIM)),
                              lambda i: (pl.multiple_of(i * cache_block + lead_rows, 8), 0))
    cache_shape = jax.ShapeDtypeStruct((rows * N_HEADS + lead_rows, V_DIM), F32)
    return pl.pallas_call(
        _project_kernel,
        grid=(rows // block_rows,),
        in_specs=[row_spec(D_MODEL), row_spec(LANES), row_spec(LANES), col_spec, col_spec,
                  full(1, D_MODEL), full(*w_n.shape), full(*w_t.shape)],
        out_specs=[pl.BlockSpec((N_HEADS, 2, V_DIM, block_rows), lambda i: (0, 0, 0, i)),
                   cache_spec,
                   pl.BlockSpec((N_HEADS, block_rows, V_DIM), lambda i: (0, i, 0)),
                   cache_spec,
                   pl.BlockSpec((N_HEADS, VT_ROWS, block_rows), lambda i: (0, 0, i)),
                   row_spec(D_POOL), row_spec(D_MODEL)],
        out_shape=[jax.ShapeDtypeStruct((N_HEADS, 2, V_DIM, rows), BF16),
                   cache_shape,
                   jax.ShapeDtypeStruct((N_HEADS, rows, V_DIM), BF16),
                   cache_shape,
                   jax.ShapeDtypeStruct((N_HEADS, VT_ROWS, rows), BF16),
                   jax.ShapeDtypeStruct((rows, D_POOL), F32),
                   jax.ShapeDtypeStruct((rows, D_MODEL), BF16)],
        compiler_params=pltpu.CompilerParams(
            dimension_semantics=("arbitrary",), vmem_limit_bytes=VMEM_LIMIT),
        name="project",
    )(x, *tables, gain, w_n, w_t)


def _cos_sin(pos):
    inv_freq = ROPE_THETA ** (-jnp.arange(0, HEAD_DIM, 2, dtype=F32) / HEAD_DIM)
    ang = pos.astype(F32)[:, None] * inv_freq[None, :]
    return jnp.cos(ang), jnp.sin(ang)


def _rope_sign():
    half = HEAD_DIM // 2
    return jnp.tile(jnp.concatenate([-jnp.ones((half,), F32), jnp.ones((half,), F32)]), LANES // HEAD_DIM)


def _rope_tables(pos):
    cos, sin = _cos_sin(pos)
    reps = 2 * LANES // HEAD_DIM
    return jnp.tile(cos, (1, reps)), jnp.tile(sin, (1, reps)) * _rope_sign(), cos.T, sin.T


def _rope_tables_range(start, n, block=256):
    assert n % block == 0
    reps = 2 * LANES // HEAD_DIM
    cos_a, sin_a = _cos_sin(start + block * jnp.arange(n // block))
    cos_b, sin_b = _cos_sin(jnp.arange(block))
    ca, sa = jnp.tile(cos_a, (1, reps))[:, None, :], jnp.tile(sin_a, (1, reps))[:, None, :]
    cb, sb = jnp.tile(cos_b, (1, reps))[None], jnp.tile(sin_b, (1, reps))[None]
    cos = (ca * cb - sa * sb).reshape(n, LANES)
    sin = ((sa * cb + ca * sb) * _rope_sign()).reshape(n, LANES)
    cat, sat = cos_a.T[:, :, None], sin_a.T[:, :, None]
    cbt, sbt = cos_b.T[:, None, :], sin_b.T[:, None, :]
    return cos, sin, (cat * cbt - sat * sbt).reshape(-1, n), (sat * cbt + cat * sbt).reshape(-1, n)


def _prompt_attn_kernel(qt_ref, k_ref, vt_ref, km_ref, vtm_ref,
                        lq1_ref, lk1_ref, lq2_ref, lk2_ref, sg_ref, o_ref,
                        m_ref, acc_ref, sa_ref, sb_ref):
    i = pl.program_id(1)
    t = ATTN_T
    qt = jnp.concatenate([qt_ref[0, 0], qt_ref[0, 1]], axis=1)

    def scores(c):
        return _dot(k_ref[0, pl.ds(pl.multiple_of(c * t, t), t), :], qt)

    def values(c):
        return vt_ref[0, :, pl.ds(pl.multiple_of(c * t, t), t)]

    def update(s, vt):
        m_prev = m_ref[...]
        m_new = jnp.maximum(m_prev, jnp.max(s, axis=0, keepdims=True))
        alpha = jnp.exp2(m_prev - m_new)
        p = jnp.exp2(s - m_new).astype(BF16)
        acc_ref[...] = alpha * acc_ref[...] + _dot(vt, p)
        m_ref[...] = m_new

    m_ref[...] = jnp.full(m_ref.shape, NEG_BIG, F32)
    acc_ref[...] = jnp.zeros(acc_ref.shape, F32)

    key = lax.broadcasted_iota(jnp.int32, (LANES, 2 * t), 0)
    s_meta = jnp.where(key < N_META, _dot(km_ref[0], qt), NEG_BIG)
    sa_ref[...] = scores(0)
    update(s_meta, vtm_ref[0])

    def run(first, count, prefetch=None, diagonal_last=False):
        bufs = (sa_ref, sb_ref)
        for n in range(count):
            cur, nxt = bufs[n % 2], bufs[(n + 1) % 2]
            if n + 1 < count:
                nxt[...] = scores(first + n + 1)
            elif prefetch is not None:
                nxt[...] = scores(prefetch)
            s = cur[...]
            if diagonal_last and n == count - 1:
                key = lax.broadcasted_iota(jnp.int32, (t, 2 * t), 0)
                qry = lax.broadcasted_iota(jnp.int32, (t, 2 * t), 1) % t
                s = jnp.where(key <= qry, s, NEG_BIG)
            update(s, values(first + n))

    group = ATTN_GROUP

    def body(n, carry):
        run(group * n, group, prefetch=group * n + group)
        return carry

    lax.fori_loop(0, i // group, body, 0)

    for rest in range(group):
        @pl.when(i % group == rest)
        def _():
            run(i - rest, rest + 1, diagonal_last=True)

    lam = _lam(lq1_ref[...], lk1_ref[...], lq2_ref[...], lk2_ref[...])
    o = acc_ref[:V_DIM, :] * (1.0 / acc_ref[V_DIM:V_DIM + 1, :])
    o = o[:, :t] - lam * o[:, t:]
    o = o * lax.rsqrt(jnp.mean(o * o, axis=0, keepdims=True) + SUBLN_EPS) * sg_ref[...]
    o_ref[...] = (o * (1.0 - _lam_init(0))).T.astype(o_ref.dtype)


def _prompt_attention(qt, kb, vt, km, vtm, lams, subln_gain_col):
    rows = kb.shape[1]
    t = ATTN_T
    assert rows % t == 0
    vec = lambda n: pl.BlockSpec((1, n), lambda h, i: (0, 0))
    head = lambda *shape: pl.BlockSpec((1,) + shape, lambda h, i: (h,) + (0,) * len(shape))
    return pl.pallas_call(
        _prompt_attn_kernel,
        grid=(N_HEADS, rows // t),
        in_specs=[pl.BlockSpec((1, 2, V_DIM, t), lambda h, i: (h, 0, 0, i)),
                  head(rows, V_DIM), head(VT_ROWS, rows), head(LANES, V_DIM), head(VT_ROWS, LANES),
                  vec(HEAD_DIM), vec(HEAD_DIM), vec(HEAD_DIM), vec(HEAD_DIM),
                  pl.BlockSpec((V_DIM, 1), lambda h, i: (0, 0))],
        out_specs=pl.BlockSpec((t, V_DIM), lambda h, i: (i, h)),
        out_shape=jax.ShapeDtypeStruct((rows, D_ATTN), BF16),
        scratch_shapes=[pltpu.VMEM((1, 2 * t), F32),
                        pltpu.VMEM((VT_ROWS, 2 * t), F32),
                        pltpu.VMEM((t, 2 * t), F32),
                        pltpu.VMEM((t, 2 * t), F32)],
        compiler_params=pltpu.CompilerParams(
            dimension_semantics=("arbitrary", "arbitrary"), vmem_limit_bytes=VMEM_LIMIT),
        name="prompt_attention",
    )(qt, kb, vt, km, vtm, *lams, subln_gain_col)


def _decode_attn_kernel(pt_ref, q_ref, kn_ref, vn_ref, *rest):
    n = PAGES_PER_STEP
    k_refs = rest[:n]
    v_refs = rest[n:2 * n]
    lq1_ref, lk1_ref, lq2_ref, lk2_ref, sg_ref, o_ref, m_ref, l_ref, acc_ref = rest[2 * n:]
    step = pl.program_id(1)
    q = q_ref[0]
    n_rows = q.shape[0]
    n_new = n_rows // (2 * N_HEADS)

    def own_head(n_keys):
        row = lax.broadcasted_iota(jnp.int32, (n_rows, n_keys), 0)
        col = lax.broadcasted_iota(jnp.int32, (n_rows, n_keys), 1)
        return row, col, (col % N_HEADS) == (row // (2 * n_new))

    @pl.when(step == 0)
    def _():
        row, col, own = own_head(kn_ref.shape[1])
        s = jnp.where(own & ((col // N_HEADS) <= (row % n_new)), _dot_nt(q, kn_ref[0]), NEG_BIG)
        m = jnp.max(s, axis=1, keepdims=True)
        p = jnp.exp2(s - m)
        m_ref[...] = jnp.broadcast_to(m, m_ref.shape)
        l_ref[...] = jnp.broadcast_to(jnp.sum(p, axis=1, keepdims=True), l_ref.shape)
        acc_ref[...] = _dot(p.astype(BF16), vn_ref[0])

    _, _, own = own_head(PAGE_SIZE * N_HEADS)
    scores = [jnp.where(own, _dot_nt(q, k_refs[r][0].astype(BF16)), NEG_BIG) for r in range(n)]
    m_prev = m_ref[...]
    l = l_ref[...]
    acc = acc_ref[...]
    for g in range(0, n, PAGE_GROUP):
        m_cur = scores[g]
        for s in scores[g + 1:g + PAGE_GROUP]:
            m_cur = jnp.maximum(m_cur, s)
        m_new = jnp.maximum(m_prev, jnp.max(m_cur, axis=1, keepdims=True))
        alpha = jnp.exp2(m_prev - m_new)
        l = alpha * l
        acc = alpha * acc
        for r in range(g, g + PAGE_GROUP):
            p = jnp.exp2(scores[r] - m_new[:, :1])
            l = l + jnp.sum(p, axis=1, keepdims=True)
            acc = acc + _dot(p.astype(BF16), v_refs[r][0].astype(BF16))
        m_prev = m_new
    m_ref[...] = m_prev
    l_ref[...] = l
    acc_ref[...] = acc

    @pl.when(step == pl.num_programs(1) - 1)
    def _():
        lam = _lam(lq1_ref[...], lk1_ref[...], lq2_ref[...], lk2_ref[...])
        o = acc_ref[...] / l_ref[...]
        o = o - lam * pltpu.roll(o, n_rows - n_new, axis=0)
        o = _rms(o, sg_ref[...], SUBLN_EPS) * (1.0 - _lam_init(0))
        for h in range(N_HEADS):
            o_ref[0, :, h * V_DIM:(h + 1) * V_DIM] = o[h * 2 * n_new:h * 2 * n_new + n_new]


def _decode_attention(page_table, q_seq, k_new, v_new, cache_k, cache_v, lams, subln_gain):
    n_seq, n_pages = page_table.shape
    n_rows = q_seq.shape[1]
    n_new = n_rows // (2 * N_HEADS)
    page_rows = PAGE_SIZE * N_HEADS
    assert n_pages % PAGES_PER_STEP == 0 and cache_k.shape[1:] == (page_rows, V_DIM)
    page_spec = lambda r: pl.BlockSpec(
        (1, page_rows, V_DIM), lambda b, s, pt: (pt[b, s * PAGES_PER_STEP + r], 0, 0))
    seq_spec = lambda *shape: pl.BlockSpec((1,) + shape, lambda b, s, pt: (b,) + (0,) * len(shape))
    vec = lambda n: pl.BlockSpec((1, n), lambda b, s, pt: (0, 0))
    grid_spec = pltpu.PrefetchScalarGridSpec(
        num_scalar_prefetch=1,
        grid=(n_seq, n_pages // PAGES_PER_STEP),
        in_specs=[seq_spec(n_rows, V_DIM), seq_spec(*k_new.shape[1:]), seq_spec(*v_new.shape[1:])]
                 + [page_spec(r) for r in range(PAGES_PER_STEP)] * 2
                 + [vec(HEAD_DIM)] * 4 + [vec(V_DIM)],
        out_specs=seq_spec(n_new, D_ATTN),
        scratch_shapes=[pltpu.VMEM((n_rows, LANES), F32),
                        pltpu.VMEM((n_rows, LANES), F32),
                        pltpu.VMEM((n_rows, V_DIM), F32)],
    )
    return pl.pallas_call(
        _decode_attn_kernel,
        grid_spec=grid_spec,
        out_shape=jax.ShapeDtypeStruct((n_seq, n_new, D_ATTN), F32),
        compiler_params=pltpu.CompilerParams(
            dimension_semantics=("arbitrary", "arbitrary"), vmem_limit_bytes=VMEM_LIMIT),
        name="decode_attention",
    )(page_table, q_seq, k_new, v_new, *([cache_k] * PAGES_PER_STEP), *([cache_v] * PAGES_PER_STEP),
      *lams, subln_gain)


def _merge_tail(attn, pool_diffs, gate_act, x, wp_ref, ps_ref, wo_ref, fg_ref):
    pool = jnp.concatenate(
        [_dot(pool_diffs[g].astype(BF16), wp_ref[g]) for g in range(len(POOL_WINDOWS))], axis=1)
    mixed = jnp.concatenate([attn.astype(F32), pool * ps_ref[...]], axis=1)
    mixed = mixed * gate_act.astype(F32)
    h = x + _dot(mixed.astype(BF16), wo_ref[...])
    return _rms(h, fg_ref[...], RMS_EPS)


def _prompt_merge_kernel(attn_ref, u_ref, um_ref, gate_ref, x_ref, wp_ref, ps_ref, wo_ref, fg_ref,
                         y_ref, ubuf):
    rows = u_ref.shape[0]

    @pl.when(pl.program_id(0) == 0)
    def _():
        ubuf[0:HALO, :] = um_ref[...]

    ubuf[HALO:HALO + rows, :] = u_ref[...]
    diffs = []
    for g, w in enumerate(POOL_WINDOWS):
        sl = slice(g * POOL_GROUP, (g + 1) * POOL_GROUP)
        total = ubuf[HALO:HALO + rows, sl]
        for back in range(1, w):
            total = total + ubuf[HALO - back:HALO - back + rows, sl]
        diffs.append(total / float(w) - ubuf[HALO:HALO + rows, sl])
    y_ref[...] = _merge_tail(attn_ref[...], diffs, gate_ref[...], x_ref[...],
                             wp_ref, ps_ref, wo_ref, fg_ref)
    ubuf[0:HALO, :] = ubuf[rows:rows + HALO, :]


def _prompt_merge(attn, u, u_meta, gate, x, wp_b, pool_scale, wo_b, final_gain):
    rows = x.shape[0]
    t = MERGE_ROWS
    assert rows % t == 0 and u_meta.shape[0] == HALO
    row_spec = lambda cols: pl.BlockSpec((t, cols), lambda i: (i, 0))
    full = lambda *shape: pl.BlockSpec(shape, lambda i: (0,) * len(shape))
    return pl.pallas_call(
        _prompt_merge_kernel,
        grid=(rows // t,),
        in_specs=[row_spec(D_ATTN), row_spec(D_POOL), full(HALO, D_POOL), row_spec(D_MODEL),
                  row_spec(D_MODEL), full(len(POOL_WINDOWS), POOL_GROUP, POOL_GROUP),
                  full(1, D_POOL), full(D_MODEL, D_MODEL), full(1, D_MODEL)],
        out_specs=row_spec(D_MODEL),
        out_shape=jax.ShapeDtypeStruct((rows, D_MODEL), F32),
        scratch_shapes=[pltpu.VMEM((t + HALO, D_POOL), F32)],
        compiler_params=pltpu.CompilerParams(
            dimension_semantics=("arbitrary",), vmem_limit_bytes=VMEM_LIMIT),
        name="prompt_merge",
    )(attn, u, u_meta, gate, x, wp_b, pool_scale, wo_b, final_gain)


def _sample_merge_kernel(attn_ref, st_ref, un_ref, gate_ref, x_ref, wp_ref, ps_ref, wo_ref, fg_ref, y_ref):
    n_state, n_seq, _ = st_ref.shape
    n_new = un_ref.shape[0] // n_seq

    def ext(t, sl):
        if t < n_state:
            return st_ref[t, :, sl]
        return un_ref[(t - n_state) * n_seq:(t - n_state + 1) * n_seq, sl]

    diffs = []
    for g, w in enumerate(POOL_WINDOWS):
        sl = slice(g * POOL_GROUP, (g + 1) * POOL_GROUP)
        per_token = []
        for j in range(n_new):
            total = ext(n_state + j, sl)
            for back in range(1, w):
                total = total + ext(n_state + j - back, sl)
            per_token.append(total / float(w) - ext(n_state + j, sl))
        diffs.append(jnp.concatenate(per_token, axis=0))
    y_ref[...] = _merge_tail(attn_ref[...], diffs, gate_ref[...], x_ref[...],
                             wp_ref, ps_ref, wo_ref, fg_ref)


def _sample_merge(attn, state_t, u_new, gate, x, wp_b, pool_scale, wo_b, final_gain):
    rows = x.shape[0]
    return pl.pallas_call(
        _sample_merge_kernel,
        out_shape=jax.ShapeDtypeStruct((rows, D_MODEL), F32),
        compiler_params=pltpu.CompilerParams(vmem_limit_bytes=VMEM_LIMIT),
        name="sample_merge",
    )(attn, state_t, u_new, gate, x, wp_b, pool_scale, wo_b, final_gain)


def kernel(x_prompt, x_sample, cache_k, cache_v, state_pool, page_table, meta_tokens, norm_gain, w_in,
           lambda_q1, lambda_k1, lambda_q2, lambda_k2, subln_gain, w_pool, pool_scale, w_out, final_gain):
    batch, seq, _ = x_prompt.shape
    n_seq, n_new, _ = x_sample.shape
    n_samp = n_seq * n_new
    assert batch == 1 and cache_k.shape[0] == 1 and N_META + n_samp <= SMALL_ROWS
    p_len = page_table.shape[1] * PAGE_SIZE
    assert N_META + 1 >= max(POOL_WINDOWS) and p_len - POOL_STATE + 1 >= max(POOL_WINDOWS)

    w_n = w_in[0, :, D_ATTN:].astype(BF16)
    w_t = w_in[0, :, :D_ATTN].T.astype(BF16)
    w_out_b = w_out[0].astype(BF16)
    w_pool_b = w_pool[0].astype(BF16)
    lams = (lambda_q1, lambda_k1, lambda_q2, lambda_k2)
    final_gain2 = final_gain[None, :]

    n_pad = SMALL_ROWS - N_META - n_samp
    pos_small = jnp.concatenate([jnp.arange(N_META), jnp.repeat(p_len + jnp.arange(n_new), n_seq),
                                 jnp.zeros((n_pad,), jnp.int32)])
    x_samp = x_sample.transpose(1, 0, 2).reshape(n_samp, D_MODEL)
    x_small = jnp.concatenate([meta_tokens, x_samp, jnp.zeros((n_pad, D_MODEL), F32)], axis=0)
    qt_s, kf_s, kb_s, vf_s, vt_s, u_s, g_s = _project(
        x_small, _rope_tables(pos_small), norm_gain, w_n, w_t, SMALL_ROWS)
    samp = slice(N_META, N_META + n_samp)
    n_meta_rows = N_META * N_HEADS
    cache_rows = slice(n_meta_rows, (N_META + n_samp) * N_HEADS)

    qt_p, kf_p, kb_p, vf_p, vt_p, u_p, g_p = _project(
        x_prompt[0], _rope_tables_range(N_META, seq), norm_gain, w_n, w_t, PROJ_ROWS,
        lead_rows=n_meta_rows)

    km = jnp.pad(kb_s[:, :N_META], ((0, 0), (0, LANES - N_META), (0, 0)))
    vtm = jnp.pad(vt_s[:, :, :N_META], ((0, 0), (0, 0), (0, LANES - N_META)))
    attn_p = _prompt_attention(qt_p, kb_p, vt_p, km, vtm, lams, subln_gain.reshape(V_DIM, 1))
    y_prompt = _prompt_merge(attn_p, u_p, u_s[:N_META], g_p, x_prompt[0], w_pool_b, pool_scale,
                             w_out_b, final_gain2)

    q_seq = qt_s[:, :, :, samp].reshape(N_HEADS, 2, V_DIM, n_new, n_seq)
    q_seq = q_seq.transpose(4, 0, 1, 3, 2).reshape(n_seq, N_HEADS * 2 * n_new, V_DIM)
    by_seq = lambda a: a.reshape(n_new, n_seq, -1).transpose(1, 0, 2)
    pad_new = ((0, 0), (0, LANES - n_new * N_HEADS), (0, 0))
    k_samp = by_seq(kf_s[cache_rows])
    v_samp = by_seq(vf_s[cache_rows])
    k_new = jnp.pad(k_samp.astype(BF16).reshape(n_seq, n_new * N_HEADS, V_DIM), pad_new)
    v_new = jnp.pad(v_samp.astype(BF16).reshape(n_seq, n_new * N_HEADS, V_DIM), pad_new)
    n_pool_pages = cache_k.shape[1]
    attn_s = _decode_attention(page_table, q_seq, k_new, v_new,
                               cache_k.reshape(n_pool_pages, PAGE_SIZE * N_HEADS, V_DIM),
                               cache_v.reshape(n_pool_pages, PAGE_SIZE * N_HEADS, V_DIM), lams, subln_gain)
    y_samp = _sample_merge(attn_s.transpose(1, 0, 2).reshape(n_samp, D_ATTN),
                           state_pool[0].transpose(1, 0, 2), u_s[samp], g_s[samp], x_samp,
                           w_pool_b, pool_scale, w_out_b, final_gain2)

    l_p = seq + N_META
    new_k_prompt = lax.dynamic_update_slice(kf_p, kf_s[:n_meta_rows], (0, 0)).reshape(1, 1, l_p, N_HEADS, V_DIM)
    new_v_prompt = lax.dynamic_update_slice(vf_p, vf_s[:n_meta_rows], (0, 0)).reshape(1, 1, l_p, N_HEADS, V_DIM)
    new_pool_prompt = u_p[-POOL_STATE:].reshape(1, 1, POOL_STATE, D_POOL)
    new_k_sample = k_samp.reshape(1, n_seq, n_new, N_HEADS, V_DIM)
    new_v_sample = v_samp.reshape(1, n_seq, n_new, N_HEADS, V_DIM)
    new_pool_sample = jnp.concatenate([state_pool[0], by_seq(u_s[samp])], axis=1)[None, :, -POOL_STATE:]
    return (y_prompt[None], by_seq(y_samp), new_k_prompt, new_v_prompt,
            new_pool_prompt, new_k_sample, new_v_sample, new_pool_sample)
```

```python
import math

import jax
import jax.numpy as jnp
from jax import lax
from jax.experimental import pallas as pl
from jax.experimental.pallas import tpu as pltpu

F32 = jnp.float32
BF16 = jnp.bfloat16

D_MODEL = 1024
N_META = 16
D_ATTN = 512
D_POOL = 512
HEAD_DIM = 64
N_HEADS = 4
V_DIM = 128
POOL_WINDOWS = (2, 4, 8, 16)
POOL_GROUP = 128
POOL_STATE = 15
D_IN = 3 * D_ATTN + D_POOL + D_MODEL
ROPE_THETA = 10000.0
RMS_EPS = 1e-6
SUBLN_EPS = 1e-5
PAGE_SIZE = 128
LANES = 128
HALO = 16
NEG_BIG = -1e30
Q_SCALE = HEAD_DIM ** -0.5 * math.log2(math.e)

PROJ_ROWS = 256
SMALL_ROWS = 256
ATTN_T = 512
ATTN_GROUP = 6
SUM_ROWS = 16
VT_ROWS = V_DIM + SUM_ROWS
MERGE_ROWS = 512
PAGES_PER_STEP = 32
PAGE_GROUP = 4
VMEM_LIMIT = 48 * 1024 * 1024


def _dot(a, b):
    return jnp.dot(a, b, preferred_element_type=F32)


def _dot_nt(a, b):
    return lax.dot_general(a, b, (((1,), (1,)), ((), ())), preferred_element_type=F32)


def _lam_init(layer):
    return 0.8 - 0.6 * math.exp(-0.3 * layer)


def _lam(lq1, lk1, lq2, lk2):
    a = jnp.sum(lq1 * lk1, axis=1, keepdims=True)
    b = jnp.sum(lq2 * lk2, axis=1, keepdims=True)
    return jnp.exp(a) - jnp.exp(b) + _lam_init(0)


def _rms(x, gain, eps):
    return x * lax.rsqrt(jnp.mean(x * x, axis=-1, keepdims=True) + eps) * gain


def _project_kernel(x_ref, cos_ref, sin_ref, gain_ref, w_ref,
                    qt_ref, kf_ref, kb_ref, vf_ref, vt_ref, u_ref, g_ref):
    rows = x_ref.shape[0]
    half = HEAD_DIM // 2
    xn = _rms(x_ref[...], gain_ref[...], RMS_EPS).astype(BF16)
    z = _dot(xn, w_ref[...])
    cos = cos_ref[...]
    sin = sin_ref[...]
    lane = lax.broadcasted_iota(jnp.int32, cos.shape, 1)
    first_half = (lane % HEAD_DIM) < half
    zeros = jnp.zeros((HEAD_DIM, rows), F32)

    def rope(xs):
        swapped = jnp.where(first_half, pltpu.roll(xs, LANES - half, axis=1), pltpu.roll(xs, half, axis=1))
        return xs * cos + swapped * sin

    for h in range(N_HEADS):
        sl = slice(h * V_DIM, (h + 1) * V_DIM)
        qt = (rope(z[:, sl]) * Q_SCALE).T
        qt_ref[h, 0] = jnp.concatenate([qt[:HEAD_DIM], zeros], axis=0).astype(BF16)
        qt_ref[h, 1] = jnp.concatenate([zeros, qt[HEAD_DIM:]], axis=0).astype(BF16)

        k = rope(z[:, D_ATTN + h * V_DIM:D_ATTN + (h + 1) * V_DIM])
        kf_ref[pl.ds(h, rows, stride=N_HEADS), :] = k
        kb_ref[h] = k.astype(BF16)

        v = z[:, 2 * D_ATTN + h * V_DIM:2 * D_ATTN + (h + 1) * V_DIM]
        vf_ref[pl.ds(h, rows, stride=N_HEADS), :] = v
        vt_ref[h, :V_DIM] = v.T.astype(BF16)
        vt_ref[h, V_DIM:] = jnp.ones((SUM_ROWS, rows), BF16)
    u_ref[...] = z[:, 3 * D_ATTN:3 * D_ATTN + D_POOL]
    gate = z[:, 3 * D_ATTN + D_POOL:]
    g_ref[...] = (gate * jax.nn.sigmoid(gate)).astype(BF16)


def _project(x, tables, gain, w_b, block_rows, lead_rows=0):
    rows = x.shape[0]
    assert rows % block_rows == 0
    row_spec = lambda cols: pl.BlockSpec((block_rows, cols), lambda i: (i, 0))
    full = lambda *shape: pl.BlockSpec(shape, lambda i: (0,) * len(shape))
    cache_block = block_rows * N_HEADS
    cache_spec = pl.BlockSpec((pl.Element(cache_block), pl.Element(V_DIM)),
                              lambda i: (pl.multiple_of(i * cache_block + lead_rows, 8), 0))
    cache_shape = jax.ShapeDtypeStruct((rows * N_HEADS + lead_rows, V_DIM), F32)
    return pl.pallas_call(
        _project_kernel,
        grid=(rows // block_rows,),
        in_specs=[row_spec(D_MODEL), row_spec(LANES), row_spec(LANES),
                  full(1, D_MODEL), full(*w_b.shape)],
        out_specs=[pl.BlockSpec((N_HEADS, 2, V_DIM, block_rows), lambda i: (0, 0, 0, i)),
                   cache_spec,
                   pl.BlockSpec((N_HEADS, block_rows, V_DIM), lambda i: (0, i, 0)),
                   cache_spec,
                   pl.BlockSpec((N_HEADS, VT_ROWS, block_rows), lambda i: (0, 0, i)),
                   row_spec(D_POOL), row_spec(D_MODEL)],
        out_shape=[jax.ShapeDtypeStruct((N_HEADS, 2, V_DIM, rows), BF16),
                   cache_shape,
                   jax.ShapeDtypeStruct((N_HEADS, rows, V_DIM), BF16),
                   cache_shape,
                   jax.ShapeDtypeStruct((N_HEADS, VT_ROWS, rows), BF16),
                   jax.ShapeDtypeStruct((rows, D_POOL), F32),
                   jax.ShapeDtypeStruct((rows, D_MODEL), BF16)],
        compiler_params=pltpu.CompilerParams(
            dimension_semantics=("arbitrary",), vmem_limit_bytes=VMEM_LIMIT),
        name="project",
    )(x, *tables, gain, w_b)


def _cos_sin(pos):
    inv_freq = ROPE_THETA ** (-jnp.arange(0, HEAD_DIM, 2, dtype=F32) / HEAD_DIM)
    ang = pos.astype(F32)[:, None] * inv_freq[None, :]
    return jnp.cos(ang), jnp.sin(ang)


def _rope_sign():
    half = HEAD_DIM // 2
    return jnp.tile(jnp.concatenate([-jnp.ones((half,), F32), jnp.ones((half,), F32)]), LANES // HEAD_DIM)


def _rope_tables(pos):
    cos, sin = _cos_sin(pos)
    reps = 2 * LANES // HEAD_DIM
    return jnp.tile(cos, (1, reps)), jnp.tile(sin, (1, reps)) * _rope_sign()


def _rope_tables_range(start, n, block=256):
    assert n % block == 0
    reps = 2 * LANES // HEAD_DIM
    cos_a, sin_a = _cos_sin(start + block * jnp.arange(n // block))
    cos_b, sin_b = _cos_sin(jnp.arange(block))
    ca, sa = jnp.tile(cos_a, (1, reps))[:, None, :], jnp.tile(sin_a, (1, reps))[:, None, :]
    cb, sb = jnp.tile(cos_b, (1, reps))[None], jnp.tile(sin_b, (1, reps))[None]
    cos = (ca * cb - sa * sb).reshape(n, LANES)
    sin = ((sa * cb + ca * sb) * _rope_sign()).reshape(n, LANES)
    return cos, sin


def _prompt_attn_kernel(qt_ref, k_ref, vt_ref, km_ref, vtm_ref,
                        lq1_ref, lk1_ref, lq2_ref, lk2_ref, sg_ref, o_ref,
                        m_ref, acc_ref, sa_ref, sb_ref):
    i = pl.program_id(1)
    t = ATTN_T
    qt = jnp.concatenate([qt_ref[0, 0], qt_ref[0, 1]], axis=1)

    def scores(c):
        return _dot(k_ref[0, pl.ds(pl.multiple_of(c * t, t), t), :], qt)

    def values(c):
        return vt_ref[0, :, pl.ds(pl.multiple_of(c * t, t), t)]

    def update(s, vt):
        m_prev = m_ref[...]
        m_new = jnp.maximum(m_prev, jnp.max(s, axis=0, keepdims=True))
        alpha = jnp.exp2(m_prev - m_new)
        p = jnp.exp2(s - m_new).astype(BF16)
        acc_ref[...] = alpha * acc_ref[...] + _dot(vt, p)
        m_ref[...] = m_new

    m_ref[...] = jnp.full(m_ref.shape, NEG_BIG, F32)
    acc_ref[...] = jnp.zeros(acc_ref.shape, F32)

    key = lax.broadcasted_iota(jnp.int32, (LANES, 2 * t), 0)
    s_meta = jnp.where(key < N_META, _dot(km_ref[0], qt), NEG_BIG)
    sa_ref[...] = scores(0)
    update(s_meta, vtm_ref[0])

    def run(first, count, prefetch=None, diagonal_last=False):
        bufs = (sa_ref, sb_ref)
        for n in range(count):
            cur, nxt = bufs[n % 2], bufs[(n + 1) % 2]
            if n + 1 < count:
                nxt[...] = scores(first + n + 1)
            elif prefetch is not None:
                nxt[...] = scores(prefetch)
            s = cur[...]
            if diagonal_last and n == count - 1:
                key = lax.broadcasted_iota(jnp.int32, (t, 2 * t), 0)
                qry = lax.broadcasted_iota(jnp.int32, (t, 2 * t), 1) % t
                s = jnp.where(key <= qry, s, NEG_BIG)
            update(s, values(first + n))

    group = ATTN_GROUP

    def body(n, carry):
        run(group * n, group, prefetch=group * n + group)
        return carry

    lax.fori_loop(0, i // group, body, 0)

    for rest in range(group):
        @pl.when(i % group == rest)
        def _():
            run(i - rest, rest + 1, diagonal_last=True)

    lam = _lam(lq1_ref[...], lk1_ref[...], lq2_ref[...], lk2_ref[...])
    o = acc_ref[:V_DIM, :] * (1.0 / acc_ref[V_DIM:V_DIM + 1, :])
    o = o[:, :t] - lam * o[:, t:]
    o = o * lax.rsqrt(jnp.mean(o * o, axis=0, keepdims=True) + SUBLN_EPS) * sg_ref[...]
    o_ref[...] = (o * (1.0 - _lam_init(0))).T.astype(o_ref.dtype)


def _prompt_attention(qt, kb, vt, km, vtm, lams, subln_gain_col):
    rows = kb.shape[1]
    t = ATTN_T
    assert rows % t == 0
    vec = lambda n: pl.BlockSpec((1, n), lambda h, i: (0, 0))
    head = lambda *shape: pl.BlockSpec((1,) + shape, lambda h, i: (h,) + (0,) * len(shape))
    return pl.pallas_call(
        _prompt_attn_kernel,
        grid=(N_HEADS, rows // t),
        in_specs=[pl.BlockSpec((1, 2, V_DIM, t), lambda h, i: (h, 0, 0, i)),
                  head(rows, V_DIM), head(VT_ROWS, rows), head(LANES, V_DIM), head(VT_ROWS, LANES),
                  vec(HEAD_DIM), vec(HEAD_DIM), vec(HEAD_DIM), vec(HEAD_DIM),
                  pl.BlockSpec((V_DIM, 1), lambda h, i: (0, 0))],
        out_specs=pl.BlockSpec((t, V_DIM), lambda h, i: (i, h)),
        out_shape=jax.ShapeDtypeStruct((rows, D_ATTN), BF16),
        scratch_shapes=[pltpu.VMEM((1, 2 * t), F32),
                        pltpu.VMEM((VT_ROWS, 2 * t), F32),
                        pltpu.VMEM((t, 2 * t), F32),
                        pltpu.VMEM((t, 2 * t), F32)],
        compiler_params=pltpu.CompilerParams(
            dimension_semantics=("arbitrary", "arbitrary"), vmem_limit_bytes=VMEM_LIMIT),
        name="prompt_attention",
    )(qt, kb, vt, km, vtm, *lams, subln_gain_col)


def _decode_attn_kernel(pt_ref, q_ref, kn_ref, vn_ref, *rest):
    n = PAGES_PER_STEP
    k_refs = rest[:n]
    v_refs = rest[n:2 * n]
    lq1_ref, lk1_ref, lq2_ref, lk2_ref, sg_ref, o_ref, m_ref, l_ref, acc_ref = rest[2 * n:]
    step = pl.program_id(1)
    q = q_ref[0]
    n_rows = q.shape[0]
    n_new = n_rows // (2 * N_HEADS)

    def own_head(n_keys):
        row = lax.broadcasted_iota(jnp.int32, (n_rows, n_keys), 0)
        col = lax.broadcasted_iota(jnp.int32, (n_rows, n_keys), 1)
        return row, col, (col % N_HEADS) == (row // (2 * n_new))

    @pl.when(step == 0)
    def _():
        row, col, own = own_head(kn_ref.shape[1])
        s = jnp.where(own & ((col // N_HEADS) <= (row % n_new)), _dot_nt(q, kn_ref[0]), NEG_BIG)
        m = jnp.max(s, axis=1, keepdims=True)
        p = jnp.exp2(s - m)
        m_ref[...] = jnp.broadcast_to(m, m_ref.shape)
        l_ref[...] = jnp.broadcast_to(jnp.sum(p, axis=1, keepdims=True), l_ref.shape)
        acc_ref[...] = _dot(p.astype(BF16), vn_ref[0])

    _, _, own = own_head(PAGE_SIZE * N_HEADS)
    scores = [jnp.where(own, _dot_nt(q, k_refs[r][0].astype(BF16)), NEG_BIG) for r in range(n)]
    m_prev = m_ref[...]
    l = l_ref[...]
    acc = acc_ref[...]
    for g in range(0, n, PAGE_GROUP):
        m_cur = scores[g]
        for s in scores[g + 1:g + PAGE_GROUP]:
            m_cur = jnp.maximum(m_cur, s)
        m_new = jnp.maximum(m_prev, jnp.max(m_cur, axis=1, keepdims=True))
        alpha = jnp.exp2(m_prev - m_new)
        l = alpha * l
        acc = alpha * acc
        for r in range(g, g + PAGE_GROUP):
            p = jnp.exp2(scores[r] - m_new[:, :1])
            l = l + jnp.sum(p, axis=1, keepdims=True)
            acc = acc + _dot(p.astype(BF16), v_refs[r][0].astype(BF16))
        m_prev = m_new
    m_ref[...] = m_prev
    l_ref[...] = l
    acc_ref[...] = acc

    @pl.when(step == pl.num_programs(1) - 1)
    def _():
        lam = _lam(lq1_ref[...], lk1_ref[...], lq2_ref[...], lk2_ref[...])
        o = acc_ref[...] / l_ref[...]
        o = o - lam * pltpu.roll(o, n_rows - n_new, axis=0)
        o = _rms(o, sg_ref[...], SUBLN_EPS) * (1.0 - _lam_init(0))
        for h in range(N_HEADS):
            o_ref[0, :, h * V_DIM:(h + 1) * V_DIM] = o[h * 2 * n_new:h * 2 * n_new + n_new]


def _decode_attention(page_table, q_seq, k_new, v_new, cache_k, cache_v, lams, subln_gain):
    n_seq, n_pages = page_table.shape
    n_rows = q_seq.shape[1]
    n_new = n_rows // (2 * N_HEADS)
    page_rows = PAGE_SIZE * N_HEADS
    assert n_pages % PAGES_PER_STEP == 0 and cache_k.shape[1:] == (page_rows, V_DIM)
    page_spec = lambda r: pl.BlockSpec(
        (1, page_rows, V_DIM), lambda b, s, pt: (pt[b, s * PAGES_PER_STEP + r], 0, 0))
    seq_spec = lambda *shape: pl.BlockSpec((1,) + shape, lambda b, s, pt: (b,) + (0,) * len(shape))
    vec = lambda n: pl.BlockSpec((1, n), lambda b, s, pt: (0, 0))
    grid_spec = pltpu.PrefetchScalarGridSpec(
        num_scalar_prefetch=1,
        grid=(n_seq, n_pages // PAGES_PER_STEP),
        in_specs=[seq_spec(n_rows, V_DIM), seq_spec(*k_new.shape[1:]), seq_spec(*v_new.shape[1:])]
                 + [page_spec(r) for r in range(PAGES_PER_STEP)] * 2
                 + [vec(HEAD_DIM)] * 4 + [vec(V_DIM)],
        out_specs=seq_spec(n_new, D_ATTN),
        scratch_shapes=[pltpu.VMEM((n_rows, LANES), F32),
                        pltpu.VMEM((n_rows, LANES), F32),
                        pltpu.VMEM((n_rows, V_DIM), F32)],
    )
    return pl.pallas_call(
        _decode_attn_kernel,
        grid_spec=grid_spec,
        out_shape=jax.ShapeDtypeStruct((n_seq, n_new, D_ATTN), F32),
        compiler_params=pltpu.CompilerParams(
            dimension_semantics=("arbitrary", "arbitrary"), vmem_limit_bytes=VMEM_LIMIT),
        name="decode_attention",
    )(page_table, q_seq, k_new, v_new, *([cache_k] * PAGES_PER_STEP), *([cache_v] * PAGES_PER_STEP),
      *lams, subln_gain)


def _merge_tail(attn, pool_diffs, gate_act, x, wp_ref, ps_ref, wo_ref, fg_ref):
    pool = jnp.concatenate(
        [_dot(pool_diffs[g].astype(BF16), wp_ref[g]) for g in range(len(POOL_WINDOWS))], axis=1)
    mixed = jnp.concatenate([attn.astype(F32), pool * ps_ref[...]], axis=1)
    mixed = mixed * gate_act.astype(F32)
    h = x + _dot(mixed.astype(BF16), wo_ref[...])
    return _rms(h, fg_ref[...], RMS_EPS)


def _prompt_merge_kernel(attn_ref, u_ref, um_ref, gate_ref, x_ref, wp_ref, ps_ref, wo_ref, fg_ref,
                         y_ref, ubuf):
    rows = u_ref.shape[0]

    @pl.when(pl.program_id(0) == 0)
    def _():
        ubuf[0:HALO, :] = um_ref[...]

    ubuf[HALO:HALO + rows, :] = u_ref[...]
    diffs = []
    for g, w in enumerate(POOL_WINDOWS):
        sl = slice(g * POOL_GROUP, (g + 1) * POOL_GROUP)
        total = ubuf[HALO:HALO + rows, sl]
        for back in range(1, w):
            total = total + ubuf[HALO - back:HALO - back + rows, sl]
        diffs.append(total / float(w) - ubuf[HALO:HALO + rows, sl])
    y_ref[...] = _merge_tail(attn_ref[...], diffs, gate_ref[...], x_ref[...],
                             wp_ref, ps_ref, wo_ref, fg_ref)
    ubuf[0:HALO, :] = ubuf[rows:rows + HALO, :]


def _prompt_merge(attn, u, u_meta, gate, x, wp_b, pool_scale, wo_b, final_gain):
    rows = x.shape[0]
    t = MERGE_ROWS
    assert rows % t == 0 and u_meta.shape[0] == HALO
    row_spec = lambda cols: pl.BlockSpec((t, cols), lambda i: (i, 0))
    full = lambda *shape: pl.BlockSpec(shape, lambda i: (0,) * len(shape))
    return pl.pallas_call(
        _prompt_merge_kernel,
        grid=(rows // t,),
        in_specs=[row_spec(D_ATTN), row_spec(D_POOL), full(HALO, D_POOL), row_spec(D_MODEL),
                  row_spec(D_MODEL), full(len(POOL_WINDOWS), POOL_GROUP, POOL_GROUP),
                  full(1, D_POOL), full(D_MODEL, D_MODEL), full(1, D_MODEL)],
        out_specs=row_spec(D_MODEL),
        out_shape=jax.ShapeDtypeStruct((rows, D_MODEL), F32),
        scratch_shapes=[pltpu.VMEM((t + HALO, D_POOL), F32)],
        compiler_params=pltpu.CompilerParams(
            dimension_semantics=("arbitrary",), vmem_limit_bytes=VMEM_LIMIT),
        name="prompt_merge",
    )(attn, u, u_meta, gate, x, wp_b, pool_scale, wo_b, final_gain)


def _sample_merge_kernel(attn_ref, st_ref, un_ref, gate_ref, x_ref, wp_ref, ps_ref, wo_ref, fg_ref, y_ref):
    n_state, n_seq, _ = st_ref.shape
    n_new = un_ref.shape[0] // n_seq

    def ext(t, sl):
        if t < n_state:
            return st_ref[t, :, sl]
        return un_ref[(t - n_state) * n_seq:(t - n_state + 1) * n_seq, sl]

    diffs = []
    for g, w in enumerate(POOL_WINDOWS):
        sl = slice(g * POOL_GROUP, (g + 1) * POOL_GROUP)
        per_token = []
        for j in range(n_new):
            total = ext(n_state + j, sl)
            for back in range(1, w):
                total = total + ext(n_state + j - back, sl)
            per_token.append(total / float(w) - ext(n_state + j, sl))
        diffs.append(jnp.concatenate(per_token, axis=0))
    y_ref[...] = _merge_tail(attn_ref[...], diffs, gate_ref[...], x_ref[...],
                             wp_ref, ps_ref, wo_ref, fg_ref)


def _sample_merge(attn, state_t, u_new, gate, x, wp_b, pool_scale, wo_b, final_gain):
    rows = x.shape[0]
    return pl.pallas_call(
        _sample_merge_kernel,
        out_shape=jax.ShapeDtypeStruct((rows, D_MODEL), F32),
        compiler_params=pltpu.CompilerParams(vmem_limit_bytes=VMEM_LIMIT),
        name="sample_merge",
    )(attn, state_t, u_new, gate, x, wp_b, pool_scale, wo_b, final_gain)


def kernel(x_prompt, x_sample, cache_k, cache_v, state_pool, page_table, meta_tokens, norm_gain, w_in,
           lambda_q1, lambda_k1, lambda_q2, lambda_k2, subln_gain, w_pool, pool_scale, w_out, final_gain):
    batch, seq, _ = x_prompt.shape
    n_seq, n_new, _ = x_sample.shape
    n_samp = n_seq * n_new
    assert batch == 1 and cache_k.shape[0] == 1 and N_META + n_samp <= SMALL_ROWS
    p_len = page_table.shape[1] * PAGE_SIZE
    assert N_META + 1 >= max(POOL_WINDOWS) and p_len - POOL_STATE + 1 >= max(POOL_WINDOWS)

    w_in_b = w_in[0].astype(BF16)
    w_out_b = w_out[0].astype(BF16)
    w_pool_b = w_pool[0].astype(BF16)
    lams = (lambda_q1, lambda_k1, lambda_q2, lambda_k2)
    final_gain2 = final_gain[None, :]

    n_pad = SMALL_ROWS - N_META - n_samp
    pos_small = jnp.concatenate([jnp.arange(N_META), jnp.repeat(p_len + jnp.arange(n_new), n_seq),
                                 jnp.zeros((n_pad,), jnp.int32)])
    x_samp = x_sample.transpose(1, 0, 2).reshape(n_samp, D_MODEL)
    x_small = jnp.concatenate([meta_tokens, x_samp, jnp.zeros((n_pad, D_MODEL), F32)], axis=0)
    qt_s, kf_s, kb_s, vf_s, vt_s, u_s, g_s = _project(
        x_small, _rope_tables(pos_small), norm_gain, w_in_b, SMALL_ROWS)
    samp = slice(N_META, N_META + n_samp)
    n_meta_rows = N_META * N_HEADS
    cache_rows = slice(n_meta_rows, (N_META + n_samp) * N_HEADS)

    qt_p, kf_p, kb_p, vf_p, vt_p, u_p, g_p = _project(
        x_prompt[0], _rope_tables_range(N_META, seq), norm_gain, w_in_b, PROJ_ROWS,
        lead_rows=n_meta_rows)

    km = jnp.pad(kb_s[:, :N_META], ((0, 0), (0, LANES - N_META), (0, 0)))
    vtm = jnp.pad(vt_s[:, :, :N_META], ((0, 0), (0, 0), (0, LANES - N_META)))
    attn_p = _prompt_attention(qt_p, kb_p, vt_p, km, vtm, lams, subln_gain.reshape(V_DIM, 1))
    y_prompt = _prompt_merge(attn_p, u_p, u_s[:N_META], g_p, x_prompt[0], w_pool_b, pool_scale,
                             w_out_b, final_gain2)

    q_seq = qt_s[:, :, :, samp].reshape(N_HEADS, 2, V_DIM, n_new, n_seq)
    q_seq = q_seq.transpose(4, 0, 1, 3, 2).reshape(n_seq, N_HEADS * 2 * n_new, V_DIM)
    by_seq = lambda a: a.reshape(n_new, n_seq, -1).transpose(1, 0, 2)
    pad_new = ((0, 0), (0, LANES - n_new * N_HEADS), (0, 0))
    k_samp = by_seq(kf_s[cache_rows])
    v_samp = by_seq(vf_s[cache_rows])
    k_new = jnp.pad(k_samp.astype(BF16).reshape(n_seq, n_new * N_HEADS, V_DIM), pad_new)
    v_new = jnp.pad(v_samp.astype(BF16).reshape(n_seq, n_new * N_HEADS, V_DIM), pad_new)
    n_pool_pages = cache_k.shape[1]
    attn_s = _decode_attention(page_table, q_seq, k_new, v_new,
                               cache_k.reshape(n_pool_pages, PAGE_SIZE * N_HEADS, V_DIM),
                               cache_v.reshape(n_pool_pages, PAGE_SIZE * N_HEADS, V_DIM), lams, subln_gain)
    y_samp = _sample_merge(attn_s.transpose(1, 0, 2).reshape(n_samp, D_ATTN),
                           state_pool[0].transpose(1, 0, 2), u_s[samp], g_s[samp], x_samp,
                           w_pool_b, pool_scale, w_out_b, final_gain2)

    l_p = seq + N_META
    new_k_prompt = lax.dynamic_update_slice(kf_p, kf_s[:n_meta_rows], (0, 0)).reshape(1, 1, l_p, N_HEADS, V_DIM)
    new_v_prompt = lax.dynamic_update_slice(vf_p, vf_s[:n_meta_rows], (0, 0)).reshape(1, 1, l_p, N_HEADS, V_DIM)
    new_pool_prompt = u_p[-POOL_STATE:].reshape(1, 1, POOL_STATE, D_POOL)
    new_k_sample = k_samp.reshape(1, n_seq, n_new, N_HEADS, V_DIM)
    new_v_sample = v_samp.reshape(1, n_seq, n_new, N_HEADS, V_DIM)
    new_pool_sample = jnp.concatenate([state_pool[0], by_seq(u_s[samp])], axis=1)[None, :, -POOL_STATE:]
    return (y_prompt[None], by_seq(y_samp), new_k_prompt, new_v_prompt,
            new_pool_prompt, new_k_sample, new_v_sample, new_pool_sample)
```

```python
import math

import jax
import jax.numpy as jnp
from jax import lax
from jax.experimental import pallas as pl
from jax.experimental.pallas import tpu as pltpu

F32 = jnp.float32
BF16 = jnp.bfloat16

D_MODEL = 1024
N_META = 16
D_ATTN = 512
D_POOL = 512
HEAD_DIM = 64
N_HEADS = 4
V_DIM = 128
POOL_WINDOWS = (2, 4, 8, 16)
POOL_GROUP = 128
POOL_STATE = 15
D_IN = 3 * D_ATTN + D_POOL + D_MODEL
ROPE_THETA = 10000.0
RMS_EPS = 1e-6
SUBLN_EPS = 1e-5
PAGE_SIZE = 128
LANES = 128
HALO = 16
NEG_BIG = -1e30
Q_SCALE = HEAD_DIM ** -0.5 * math.log2(math.e)

PROJ_ROWS = 512
SMALL_ROWS = 256
ATTN_T = 512
ATTN_GROUP = 8
SUM_ROWS = 16
VT_ROWS = V_DIM + SUM_ROWS
MERGE_ROWS = 512
PAGES_PER_STEP = 32
PAGE_GROUP = 4
VMEM_LIMIT = 48 * 1024 * 1024


def _dot(a, b):
    return jnp.dot(a, b, preferred_element_type=F32)


def _dot_nt(a, b):
    return lax.dot_general(a, b, (((1,), (1,)), ((), ())), preferred_element_type=F32)


def _lam_init(layer):
    return 0.8 - 0.6 * math.exp(-0.3 * layer)


def _lam(lq1, lk1, lq2, lk2):
    a = jnp.sum(lq1 * lk1, axis=1, keepdims=True)
    b = jnp.sum(lq2 * lk2, axis=1, keepdims=True)
    return jnp.exp(a) - jnp.exp(b) + _lam_init(0)


def _rms(x, gain, eps):
    return x * lax.rsqrt(jnp.mean(x * x, axis=-1, keepdims=True) + eps) * gain


def _project_kernel(x_ref, cos_ref, sin_ref, gain_ref, w_ref,
                    qt_ref, kf_ref, kb_ref, vf_ref, vt_ref, u_ref, g_ref):
    rows = x_ref.shape[0]
    half = HEAD_DIM // 2
    xn = _rms(x_ref[...], gain_ref[...], RMS_EPS).astype(BF16)
    z = _dot(xn, w_ref[...])
    cos = cos_ref[...]
    sin = sin_ref[...]
    lane = lax.broadcasted_iota(jnp.int32, cos.shape, 1)
    first_half = (lane % HEAD_DIM) < half
    zeros = jnp.zeros((HEAD_DIM, rows), F32)

    def rope(xs):
        swapped = jnp.where(first_half, pltpu.roll(xs, LANES - half, axis=1), pltpu.roll(xs, half, axis=1))
        return xs * cos + swapped * sin

    for h in range(N_HEADS):
        sl = slice(h * V_DIM, (h + 1) * V_DIM)
        qt = (rope(z[:, sl]) * Q_SCALE).T
        qt_ref[h, 0] = jnp.concatenate([qt[:HEAD_DIM], zeros], axis=0).astype(BF16)
        qt_ref[h, 1] = jnp.concatenate([zeros, qt[HEAD_DIM:]], axis=0).astype(BF16)

        k = rope(z[:, D_ATTN + h * V_DIM:D_ATTN + (h + 1) * V_DIM])
        kf_ref[pl.ds(h, rows, stride=N_HEADS), :] = k
        kb_ref[h] = k.astype(BF16)

        v = z[:, 2 * D_ATTN + h * V_DIM:2 * D_ATTN + (h + 1) * V_DIM]
        vf_ref[pl.ds(h, rows, stride=N_HEADS), :] = v
        vt_ref[h, :V_DIM] = v.T.astype(BF16)
        vt_ref[h, V_DIM:] = jnp.ones((SUM_ROWS, rows), BF16)
    u_ref[...] = z[:, 3 * D_ATTN:3 * D_ATTN + D_POOL]
    gate = z[:, 3 * D_ATTN + D_POOL:]
    g_ref[...] = (gate * jax.nn.sigmoid(gate)).astype(BF16)


def _project(x, tables, gain, w_b, block_rows, lead_rows=0):
    rows = x.shape[0]
    assert rows % block_rows == 0
    row_spec = lambda cols: pl.BlockSpec((block_rows, cols), lambda i: (i, 0))
    full = lambda *shape: pl.BlockSpec(shape, lambda i: (0,) * len(shape))
    cache_block = block_rows * N_HEADS
    cache_spec = pl.BlockSpec((pl.Element(cache_block), pl.Element(V_DIM)),
                              lambda i: (pl.multiple_of(i * cache_block + lead_rows, 8), 0))
    cache_shape = jax.ShapeDtypeStruct((rows * N_HEADS + lead_rows, V_DIM), F32)
    return pl.pallas_call(
        _project_kernel,
        grid=(rows // block_rows,),
        in_specs=[row_spec(D_MODEL), row_spec(LANES), row_spec(LANES),
                  full(1, D_MODEL), full(*w_b.shape)],
        out_specs=[pl.BlockSpec((N_HEADS, 2, V_DIM, block_rows), lambda i: (0, 0, 0, i)),
                   cache_spec,
                   pl.BlockSpec((N_HEADS, block_rows, V_DIM), lambda i: (0, i, 0)),
                   cache_spec,
                   pl.BlockSpec((N_HEADS, VT_ROWS, block_rows), lambda i: (0, 0, i)),
                   row_spec(D_POOL), row_spec(D_MODEL)],
        out_shape=[jax.ShapeDtypeStruct((N_HEADS, 2, V_DIM, rows), BF16),
                   cache_shape,
                   jax.ShapeDtypeStruct((N_HEADS, rows, V_DIM), BF16),
                   cache_shape,
                   jax.ShapeDtypeStruct((N_HEADS, VT_ROWS, rows), BF16),
                   jax.ShapeDtypeStruct((rows, D_POOL), F32),
                   jax.ShapeDtypeStruct((rows, D_MODEL), BF16)],
        compiler_params=pltpu.CompilerParams(
            dimension_semantics=("arbitrary",), vmem_limit_bytes=VMEM_LIMIT),
        name="project",
    )(x, *tables, gain, w_b)


def _cos_sin(pos):
    inv_freq = ROPE_THETA ** (-jnp.arange(0, HEAD_DIM, 2, dtype=F32) / HEAD_DIM)
    ang = pos.astype(F32)[:, None] * inv_freq[None, :]
    return jnp.cos(ang), jnp.sin(ang)


def _rope_sign():
    half = HEAD_DIM // 2
    return jnp.tile(jnp.concatenate([-jnp.ones((half,), F32), jnp.ones((half,), F32)]), LANES // HEAD_DIM)


def _rope_tables(pos):
    cos, sin = _cos_sin(pos)
    reps = 2 * LANES // HEAD_DIM
    return jnp.tile(cos, (1, reps)), jnp.tile(sin, (1, reps)) * _rope_sign()


def _rope_tables_range(start, n, block=256):
    assert n % block == 0
    reps = 2 * LANES // HEAD_DIM
    cos_a, sin_a = _cos_sin(start + block * jnp.arange(n // block))
    cos_b, sin_b = _cos_sin(jnp.arange(block))
    ca, sa = jnp.tile(cos_a, (1, reps))[:, None, :], jnp.tile(sin_a, (1, reps))[:, None, :]
    cb, sb = jnp.tile(cos_b, (1, reps))[None], jnp.tile(sin_b, (1, reps))[None]
    cos = (ca * cb - sa * sb).reshape(n, LANES)
    sin = ((sa * cb + ca * sb) * _rope_sign()).reshape(n, LANES)
    return cos, sin


def _prompt_attn_kernel(qt_ref, k_ref, vt_ref, km_ref, vtm_ref,
                        lq1_ref, lk1_ref, lq2_ref, lk2_ref, sg_ref, o_ref,
                        m_ref, acc_ref, sa_ref, sb_ref):
    i = pl.program_id(1)
    t = ATTN_T
    qt = jnp.concatenate([qt_ref[0, 0], qt_ref[0, 1]], axis=1)

    def scores(c):
        return _dot(k_ref[0, pl.ds(pl.multiple_of(c * t, t), t), :], qt)

    def values(c):
        return vt_ref[0, :, pl.ds(pl.multiple_of(c * t, t), t)]

    def update(s, vt):
        m_prev = m_ref[...]
        m_new = jnp.maximum(m_prev, jnp.max(s, axis=0, keepdims=True))
        alpha = jnp.exp2(m_prev - m_new)
        p = jnp.exp2(s - m_new).astype(BF16)
        acc_ref[...] = alpha * acc_ref[...] + _dot(vt, p)
        m_ref[...] = m_new

    m_ref[...] = jnp.full(m_ref.shape, NEG_BIG, F32)
    acc_ref[...] = jnp.zeros(acc_ref.shape, F32)

    key = lax.broadcasted_iota(jnp.int32, (LANES, 2 * t), 0)
    s_meta = jnp.where(key < N_META, _dot(km_ref[0], qt), NEG_BIG)
    sa_ref[...] = scores(0)
    update(s_meta, vtm_ref[0])

    def run(first, count, prefetch=None, diagonal_last=False):
        bufs = (sa_ref, sb_ref)
        for n in range(count):
            cur, nxt = bufs[n % 2], bufs[(n + 1) % 2]
            if n + 1 < count:
                nxt[...] = scores(first + n + 1)
            elif prefetch is not None:
                nxt[...] = scores(prefetch)
            s = cur[...]
            if diagonal_last and n == count - 1:
                key = lax.broadcasted_iota(jnp.int32, (t, 2 * t), 0)
                qry = lax.broadcasted_iota(jnp.int32, (t, 2 * t), 1) % t
                s = jnp.where(key <= qry, s, NEG_BIG)
            update(s, values(first + n))

    group = ATTN_GROUP

    def body(n, carry):
        run(group * n, group, prefetch=group * n + group)
        return carry

    lax.fori_loop(0, i // group, body, 0)

    for rest in range(group):
        @pl.when(i % group == rest)
        def _():
            run(i - rest, rest + 1, diagonal_last=True)

    lam = _lam(lq1_ref[...], lk1_ref[...], lq2_ref[...], lk2_ref[...])
    o = acc_ref[:V_DIM, :] * (1.0 / acc_ref[V_DIM:V_DIM + 1, :])
    o = o[:, :t] - lam * o[:, t:]
    o = o * lax.rsqrt(jnp.mean(o * o, axis=0, keepdims=True) + SUBLN_EPS) * sg_ref[...]
    o_ref[...] = (o * (1.0 - _lam_init(0))).T.astype(o_ref.dtype)


def _prompt_attention(qt, kb, vt, km, vtm, lams, subln_gain_col):
    rows = kb.shape[1]
    t = ATTN_T
    assert rows % t == 0
    vec = lambda n: pl.BlockSpec((1, n), lambda h, i: (0, 0))
    head = lambda *shape: pl.BlockSpec((1,) + shape, lambda h, i: (h,) + (0,) * len(shape))
    return pl.pallas_call(
        _prompt_attn_kernel,
        grid=(N_HEADS, rows // t),
        in_specs=[pl.BlockSpec((1, 2, V_DIM, t), lambda h, i: (h, 0, 0, i)),
                  head(rows, V_DIM), head(VT_ROWS, rows), head(LANES, V_DIM), head(VT_ROWS, LANES),
                  vec(HEAD_DIM), vec(HEAD_DIM), vec(HEAD_DIM), vec(HEAD_DIM),
                  pl.BlockSpec((V_DIM, 1), lambda h, i: (0, 0))],
        out_specs=pl.BlockSpec((t, V_DIM), lambda h, i: (i, h)),
        out_shape=jax.ShapeDtypeStruct((rows, D_ATTN), BF16),
        scratch_shapes=[pltpu.VMEM((1, 2 * t), F32),
                        pltpu.VMEM((VT_ROWS, 2 * t), F32),
                        pltpu.VMEM((t, 2 * t), F32),
                        pltpu.VMEM((t, 2 * t), F32)],
        compiler_params=pltpu.CompilerParams(
            dimension_semantics=("arbitrary", "arbitrary"), vmem_limit_bytes=VMEM_LIMIT),
        name="prompt_attention",
    )(qt, kb, vt, km, vtm, *lams, subln_gain_col)


def _decode_attn_kernel(pt_ref, q_ref, kn_ref, vn_ref, *rest):
    n = PAGES_PER_STEP
    k_refs = rest[:n]
    v_refs = rest[n:2 * n]
    lq1_ref, lk1_ref, lq2_ref, lk2_ref, sg_ref, o_ref, m_ref, l_ref, acc_ref = rest[2 * n:]
    step = pl.program_id(1)
    q = q_ref[0]
    n_rows = q.shape[0]
    n_new = n_rows // (2 * N_HEADS)

    def own_head(n_keys):
        row = lax.broadcasted_iota(jnp.int32, (n_rows, n_keys), 0)
        col = lax.broadcasted_iota(jnp.int32, (n_rows, n_keys), 1)
        return row, col, (col % N_HEADS) == (row // (2 * n_new))

    @pl.when(step == 0)
    def _():
        row, col, own = own_head(kn_ref.shape[1])
        s = jnp.where(own & ((col // N_HEADS) <= (row % n_new)), _dot_nt(q, kn_ref[0]), NEG_BIG)
        m = jnp.max(s, axis=1, keepdims=True)
        p = jnp.exp2(s - m)
        m_ref[...] = jnp.broadcast_to(m, m_ref.shape)
        l_ref[...] = jnp.broadcast_to(jnp.sum(p, axis=1, keepdims=True), l_ref.shape)
        acc_ref[...] = _dot(p.astype(BF16), vn_ref[0])

    _, _, own = own_head(PAGE_SIZE * N_HEADS)
    scores = [jnp.where(own, _dot_nt(q, k_refs[r][0].astype(BF16)), NEG_BIG) for r in range(n)]
    m_prev = m_ref[...]
    l = l_ref[...]
    acc = acc_ref[...]
    for g in range(0, n, PAGE_GROUP):
        m_cur = scores[g]
        for s in scores[g + 1:g + PAGE_GROUP]:
            m_cur = jnp.maximum(m_cur, s)
        m_new = jnp.maximum(m_prev, jnp.max(m_cur, axis=1, keepdims=True))
        alpha = jnp.exp2(m_prev - m_new)
        l = alpha * l
        acc = alpha * acc
        for r in range(g, g + PAGE_GROUP):
            p = jnp.exp2(scores[r] - m_new[:, :1])
            l = l + jnp.sum(p, axis=1, keepdims=True)
            acc = acc + _dot(p.astype(BF16), v_refs[r][0].astype(BF16))
        m_prev = m_new
    m_ref[...] = m_prev
    l_ref[...] = l
    acc_ref[...] = acc

    @pl.when(step == pl.num_programs(1) - 1)
    def _():
        lam = _lam(lq1_ref[...], lk1_ref[...], lq2_ref[...], lk2_ref[...])
        o = acc_ref[...] / l_ref[...]
        o = o - lam * pltpu.roll(o, n_rows - n_new, axis=0)
        o = _rms(o, sg_ref[...], SUBLN_EPS) * (1.0 - _lam_init(0))
        for h in range(N_HEADS):
            o_ref[0, :, h * V_DIM:(h + 1) * V_DIM] = o[h * 2 * n_new:h * 2 * n_new + n_new]


def _decode_attention(page_table, q_seq, k_new, v_new, cache_k, cache_v, lams, subln_gain):
    n_seq, n_pages = page_table.shape
    n_rows = q_seq.shape[1]
    n_new = n_rows // (2 * N_HEADS)
    page_rows = PAGE_SIZE * N_HEADS
    assert n_pages % PAGES_PER_STEP == 0 and cache_k.shape[1:] == (page_rows, V_DIM)
    page_spec = lambda r: pl.BlockSpec(
        (1, page_rows, V_DIM), lambda b, s, pt: (pt[b, s * PAGES_PER_STEP + r], 0, 0))
    seq_spec = lambda *shape: pl.BlockSpec((1,) + shape, lambda b, s, pt: (b,) + (0,) * len(shape))
    vec = lambda n: pl.BlockSpec((1, n), lambda b, s, pt: (0, 0))
    grid_spec = pltpu.PrefetchScalarGridSpec(
        num_scalar_prefetch=1,
        grid=(n_seq, n_pages // PAGES_PER_STEP),
        in_specs=[seq_spec(n_rows, V_DIM), seq_spec(*k_new.shape[1:]), seq_spec(*v_new.shape[1:])]
                 + [page_spec(r) for r in range(PAGES_PER_STEP)] * 2
                 + [vec(HEAD_DIM)] * 4 + [vec(V_DIM)],
        out_specs=seq_spec(n_new, D_ATTN),
        scratch_shapes=[pltpu.VMEM((n_rows, LANES), F32),
                        pltpu.VMEM((n_rows, LANES), F32),
                        pltpu.VMEM((n_rows, V_DIM), F32)],
    )
    return pl.pallas_call(
        _decode_attn_kernel,
        grid_spec=grid_spec,
        out_shape=jax.ShapeDtypeStruct((n_seq, n_new, D_ATTN), F32),
        compiler_params=pltpu.CompilerParams(
            dimension_semantics=("arbitrary", "arbitrary"), vmem_limit_bytes=VMEM_LIMIT),
        name="decode_attention",
    )(page_table, q_seq, k_new, v_new, *([cache_k] * PAGES_PER_STEP), *([cache_v] * PAGES_PER_STEP),
      *lams, subln_gain)


def _merge_tail(attn, pool_diffs, gate_act, x, wp_ref, ps_ref, wo_ref, fg_ref):
    pool = jnp.concatenate(
        [_dot(pool_diffs[g].astype(BF16), wp_ref[g]) for g in range(len(POOL_WINDOWS))], axis=1)
    mixed = jnp.concatenate([attn.astype(F32), pool * ps_ref[...]], axis=1)
    mixed = mixed * gate_act.astype(F32)
    h = x + _dot(mixed.astype(BF16), wo_ref[...])
    return _rms(h, fg_ref[...], RMS_EPS)


def _prompt_merge_kernel(attn_ref, u_ref, um_ref, gate_ref, x_ref, wp_ref, ps_ref, wo_ref, fg_ref,
                         y_ref, ubuf):
    rows = u_ref.shape[0]

    @pl.when(pl.program_id(0) == 0)
    def _():
        ubuf[0:HALO, :] = um_ref[...]

    ubuf[HALO:HALO + rows, :] = u_ref[...]
    diffs = []
    for g, w in enumerate(POOL_WINDOWS):
        sl = slice(g * POOL_GROUP, (g + 1) * POOL_GROUP)
        total = ubuf[HALO:HALO + rows, sl]
        for back in range(1, w):
            total = total + ubuf[HALO - back:HALO - back + rows, sl]
        diffs.append(total / float(w) - ubuf[HALO:HALO + rows, sl])
    y_ref[...] = _merge_tail(attn_ref[...], diffs, gate_ref[...], x_ref[...],
                             wp_ref, ps_ref, wo_ref, fg_ref)
    ubuf[0:HALO, :] = ubuf[rows:rows + HALO, :]


def _prompt_merge(attn, u, u_meta, gate, x, wp_b, pool_scale, wo_b, final_gain):
    rows = x.shape[0]
    t = MERGE_ROWS
    assert rows % t == 0 and u_meta.shape[0] == HALO
    row_spec = lambda cols: pl.BlockSpec((t, cols), lambda i: (i, 0))
    full = lambda *shape: pl.BlockSpec(shape, lambda i: (0,) * len(shape))
    return pl.pallas_call(
        _prompt_merge_kernel,
        grid=(rows // t,),
        in_specs=[row_spec(D_ATTN), row_spec(D_POOL), full(HALO, D_POOL), row_spec(D_MODEL),
                  row_spec(D_MODEL), full(len(POOL_WINDOWS), POOL_GROUP, POOL_GROUP),
                  full(1, D_POOL), full(D_MODEL, D_MODEL), full(1, D_MODEL)],
        out_specs=row_spec(D_MODEL),
        out_shape=jax.ShapeDtypeStruct((rows, D_MODEL), F32),
        scratch_shapes=[pltpu.VMEM((t + HALO, D_POOL), F32)],
        compiler_params=pltpu.CompilerParams(
            dimension_semantics=("arbitrary",), vmem_limit_bytes=VMEM_LIMIT),
        name="prompt_merge",
    )(attn, u, u_meta, gate, x, wp_b, pool_scale, wo_b, final_gain)


def _sample_merge_kernel(attn_ref, st_ref, un_ref, gate_ref, x_ref, wp_ref, ps_ref, wo_ref, fg_ref, y_ref):
    n_state, n_seq, _ = st_ref.shape
    n_new = un_ref.shape[0] // n_seq

    def ext(t, sl):
        if t < n_state:
            return st_ref[t, :, sl]
        return un_ref[(t - n_state) * n_seq:(t - n_state + 1) * n_seq, sl]

    diffs = []
    for g, w in enumerate(POOL_WINDOWS):
        sl = slice(g * POOL_GROUP, (g + 1) * POOL_GROUP)
        per_token = []
        for j in range(n_new):
            total = ext(n_state + j, sl)
            for back in range(1, w):
                total = total + ext(n_state + j - back, sl)
            per_token.append(total / float(w) - ext(n_state + j, sl))
        diffs.append(jnp.concatenate(per_token, axis=0))
    y_ref[...] = _merge_tail(attn_ref[...], diffs, gate_ref[...], x_ref[...],
                             wp_ref, ps_ref, wo_ref, fg_ref)


def _sample_merge(attn, state_t, u_new, gate, x, wp_b, pool_scale, wo_b, final_gain):
    rows = x.shape[0]
    return pl.pallas_call(
        _sample_merge_kernel,
        out_shape=jax.ShapeDtypeStruct((rows, D_MODEL), F32),
        compiler_params=pltpu.CompilerParams(vmem_limit_bytes=VMEM_LIMIT),
        name="sample_merge",
    )(attn, state_t, u_new, gate, x, wp_b, pool_scale, wo_b, final_gain)


def kernel(x_prompt, x_sample, cache_k, cache_v, state_pool, page_table, meta_tokens, norm_gain, w_in,
           lambda_q1, lambda_k1, lambda_q2, lambda_k2, subln_gain, w_pool, pool_scale, w_out, final_gain):
    batch, seq, _ = x_prompt.shape
    n_seq, n_new, _ = x_sample.shape
    n_samp = n_seq * n_new
    assert batch == 1 and cache_k.shape[0] == 1 and N_META + n_samp <= SMALL_ROWS
    p_len = page_table.shape[1] * PAGE_SIZE
    assert N_META + 1 >= max(POOL_WINDOWS) and p_len - POOL_STATE + 1 >= max(POOL_WINDOWS)

    w_in_b = w_in[0].astype(BF16)
    w_out_b = w_out[0].astype(BF16)
    w_pool_b = w_pool[0].astype(BF16)
    lams = (lambda_q1, lambda_k1, lambda_q2, lambda_k2)
    final_gain2 = final_gain[None, :]

    n_pad = SMALL_ROWS - N_META - n_samp
    pos_small = jnp.concatenate([jnp.arange(N_META), jnp.repeat(p_len + jnp.arange(n_new), n_seq),
                                 jnp.zeros((n_pad,), jnp.int32)])
    x_samp = x_sample.transpose(1, 0, 2).reshape(n_samp, D_MODEL)
    x_small = jnp.concatenate([meta_tokens, x_samp, jnp.zeros((n_pad, D_MODEL), F32)], axis=0)
    qt_s, kf_s, kb_s, vf_s, vt_s, u_s, g_s = _project(
        x_small, _rope_tables(pos_small), norm_gain, w_in_b, SMALL_ROWS)
    samp = slice(N_META, N_META + n_samp)
    n_meta_rows = N_META * N_HEADS
    cache_rows = slice(n_meta_rows, (N_META + n_samp) * N_HEADS)

    qt_p, kf_p, kb_p, vf_p, vt_p, u_p, g_p = _project(
        x_prompt[0], _rope_tables_range(N_META, seq), norm_gain, w_in_b, PROJ_ROWS,
        lead_rows=n_meta_rows)

    km = jnp.pad(kb_s[:, :N_META], ((0, 0), (0, LANES - N_META), (0, 0)))
    vtm = jnp.pad(vt_s[:, :, :N_META], ((0, 0), (0, 0), (0, LANES - N_META)))
    attn_p = _prompt_attention(qt_p, kb_p, vt_p, km, vtm, lams, subln_gain.reshape(V_DIM, 1))
    y_prompt = _prompt_merge(attn_p, u_p, u_s[:N_META], g_p, x_prompt[0], w_pool_b, pool_scale,
                             w_out_b, final_gain2)

    q_seq = qt_s[:, :, :, samp].reshape(N_HEADS, 2, V_DIM, n_new, n_seq)
    q_seq = q_seq.transpose(4, 0, 1, 3, 2).reshape(n_seq, N_HEADS * 2 * n_new, V_DIM)
    by_seq = lambda a: a.reshape(n_new, n_seq, -1).transpose(1, 0, 2)
    pad_new = ((0, 0), (0, LANES - n_new * N_HEADS), (0, 0))
    k_samp = by_seq(kf_s[cache_rows])
    v_samp = by_seq(vf_s[cache_rows])
    k_new = jnp.pad(k_samp.astype(BF16).reshape(n_seq, n_new * N_HEADS, V_DIM), pad_new)
    v_new = jnp.pad(v_samp.astype(BF16).reshape(n_seq, n_new * N_HEADS, V_DIM), pad_new)
    n_pool_pages = cache_k.shape[1]
    attn_s = _decode_attention(page_table, q_seq, k_new, v_new,
                               cache_k.reshape(n_pool_pages, PAGE_SIZE * N_HEADS, V_DIM),
                               cache_v.reshape(n_pool_pages, PAGE_SIZE * N_HEADS, V_DIM), lams, subln_gain)
    y_samp = _sample_merge(attn_s.transpose(1, 0, 2).reshape(n_samp, D_ATTN),
                           state_pool[0].transpose(1, 0, 2), u_s[samp], g_s[samp], x_samp,
                           w_pool_b, pool_scale, w_out_b, final_gain2)

    l_p = seq + N_META
    new_k_prompt = lax.dynamic_update_slice(kf_p, kf_s[:n_meta_rows], (0, 0)).reshape(1, 1, l_p, N_HEADS, V_DIM)
    new_v_prompt = lax.dynamic_update_slice(vf_p, vf_s[:n_meta_rows], (0, 0)).reshape(1, 1, l_p, N_HEADS, V_DIM)
    new_pool_prompt = u_p[-POOL_STATE:].reshape(1, 1, POOL_STATE, D_POOL)
    new_k_sample = k_samp.reshape(1, n_seq, n_new, N_HEADS, V_DIM)
    new_v_sample = v_samp.reshape(1, n_seq, n_new, N_HEADS, V_DIM)
    new_pool_sample = jnp.concatenate([state_pool[0], by_seq(u_s[samp])], axis=1)[None, :, -POOL_STATE:]
    return (y_prompt[None], by_seq(y_samp), new_k_prompt, new_v_prompt,
            new_pool_prompt, new_k_sample, new_v_sample, new_pool_sample)
```

```python
import math

import jax
import jax.numpy as jnp
from jax import lax
from jax.experimental import pallas as pl
from jax.experimental.pallas import tpu as pltpu

F32 = jnp.float32
BF16 = jnp.bfloat16

D_MODEL = 1024
N_META = 16
D_ATTN = 512
D_POOL = 512
HEAD_DIM = 64
N_HEADS = 4
V_DIM = 128
POOL_WINDOWS = (2, 4, 8, 16)
POOL_GROUP = 128
POOL_STATE = 15
D_IN = 3 * D_ATTN + D_POOL + D_MODEL
ROPE_THETA = 10000.0
RMS_EPS = 1e-6
SUBLN_EPS = 1e-5
PAGE_SIZE = 128
LANES = 128
HALO = 16
NEG_BIG = -1e30
Q_SCALE = HEAD_DIM ** -0.5 * math.log2(math.e)

PROJ_ROWS = 512
SMALL_ROWS = 256
ATTN_T = 512
ATTN_GROUP = 8
SUM_ROWS = 16
VT_ROWS = V_DIM + SUM_ROWS
MERGE_ROWS = 1024
PAGES_PER_STEP = 32
PAGE_GROUP = 4
VMEM_LIMIT = 48 * 1024 * 1024


def _dot(a, b):
    return jnp.dot(a, b, preferred_element_type=F32)


def _dot_nt(a, b):
    return lax.dot_general(a, b, (((1,), (1,)), ((), ())), preferred_element_type=F32)


def _lam_init(layer):
    return 0.8 - 0.6 * math.exp(-0.3 * layer)


def _lam(lq1, lk1, lq2, lk2):
    a = jnp.sum(lq1 * lk1, axis=1, keepdims=True)
    b = jnp.sum(lq2 * lk2, axis=1, keepdims=True)
    return jnp.exp(a) - jnp.exp(b) + _lam_init(0)


def _rms(x, gain, eps):
    return x * lax.rsqrt(jnp.mean(x * x, axis=-1, keepdims=True) + eps) * gain


def _project_kernel(x_ref, cos_ref, sin_ref, gain_ref, w_ref,
                    qt_ref, kf_ref, kb_ref, vf_ref, vt_ref, u_ref, g_ref):
    rows = x_ref.shape[0]
    half = HEAD_DIM // 2
    xn = _rms(x_ref[...], gain_ref[...], RMS_EPS).astype(BF16)
    z = _dot(xn, w_ref[...])
    cos = cos_ref[...]
    sin = sin_ref[...]
    lane = lax.broadcasted_iota(jnp.int32, cos.shape, 1)
    first_half = (lane % HEAD_DIM) < half
    zeros = jnp.zeros((HEAD_DIM, rows), F32)

    def rope(xs):
        swapped = jnp.where(first_half, pltpu.roll(xs, LANES - half, axis=1), pltpu.roll(xs, half, axis=1))
        return xs * cos + swapped * sin

    for h in range(N_HEADS):
        sl = slice(h * V_DIM, (h + 1) * V_DIM)
        qt = (rope(z[:, sl]) * Q_SCALE).T
        qt_ref[h, 0] = jnp.concatenate([qt[:HEAD_DIM], zeros], axis=0).astype(BF16)
        qt_ref[h, 1] = jnp.concatenate([zeros, qt[HEAD_DIM:]], axis=0).astype(BF16)

        k = rope(z[:, D_ATTN + h * V_DIM:D_ATTN + (h + 1) * V_DIM])
        kf_ref[pl.ds(h, rows, stride=N_HEADS), :] = k
        kb_ref[h] = k.astype(BF16)

        v = z[:, 2 * D_ATTN + h * V_DIM:2 * D_ATTN + (h + 1) * V_DIM]
        vf_ref[pl.ds(h, rows, stride=N_HEADS), :] = v
        vt_ref[h, :V_DIM] = v.T.astype(BF16)
        vt_ref[h, V_DIM:] = jnp.ones((SUM_ROWS, rows), BF16)
    u_ref[...] = z[:, 3 * D_ATTN:3 * D_ATTN + D_POOL]
    gate = z[:, 3 * D_ATTN + D_POOL:]
    g_ref[...] = (gate * jax.nn.sigmoid(gate)).astype(BF16)


def _project(x, tables, gain, w_b, block_rows, lead_rows=0):
    rows = x.shape[0]
    assert rows % block_rows == 0
    row_spec = lambda cols: pl.BlockSpec((block_rows, cols), lambda i: (i, 0))
    full = lambda *shape: pl.BlockSpec(shape, lambda i: (0,) * len(shape))
    cache_block = block_rows * N_HEADS
    cache_spec = pl.BlockSpec((pl.Element(cache_block), pl.Element(V_DIM)),
                              lambda i: (pl.multiple_of(i * cache_block + lead_rows, 8), 0))
    cache_shape = jax.ShapeDtypeStruct((rows * N_HEADS + lead_rows, V_DIM), F32)
    return pl.pallas_call(
        _project_kernel,
        grid=(rows // block_rows,),
        in_specs=[row_spec(D_MODEL), row_spec(LANES), row_spec(LANES),
                  full(1, D_MODEL), full(*w_b.shape)],
        out_specs=[pl.BlockSpec((N_HEADS, 2, V_DIM, block_rows), lambda i: (0, 0, 0, i)),
                   cache_spec,
                   pl.BlockSpec((N_HEADS, block_rows, V_DIM), lambda i: (0, i, 0)),
                   cache_spec,
                   pl.BlockSpec((N_HEADS, VT_ROWS, block_rows), lambda i: (0, 0, i)),
                   row_spec(D_POOL), row_spec(D_MODEL)],
        out_shape=[jax.ShapeDtypeStruct((N_HEADS, 2, V_DIM, rows), BF16),
                   cache_shape,
                   jax.ShapeDtypeStruct((N_HEADS, rows, V_DIM), BF16),
                   cache_shape,
                   jax.ShapeDtypeStruct((N_HEADS, VT_ROWS, rows), BF16),
                   jax.ShapeDtypeStruct((rows, D_POOL), F32),
                   jax.ShapeDtypeStruct((rows, D_MODEL), BF16)],
        compiler_params=pltpu.CompilerParams(
            dimension_semantics=("arbitrary",), vmem_limit_bytes=VMEM_LIMIT),
        name="project",
    )(x, *tables, gain, w_b)


def _cos_sin(pos):
    inv_freq = ROPE_THETA ** (-jnp.arange(0, HEAD_DIM, 2, dtype=F32) / HEAD_DIM)
    ang = pos.astype(F32)[:, None] * inv_freq[None, :]
    return jnp.cos(ang), jnp.sin(ang)


def _rope_sign():
    half = HEAD_DIM // 2
    return jnp.tile(jnp.concatenate([-jnp.ones((half,), F32), jnp.ones((half,), F32)]), LANES // HEAD_DIM)


def _rope_tables(pos):
    cos, sin = _cos_sin(pos)
    reps = 2 * LANES // HEAD_DIM
    return jnp.tile(cos, (1, reps)), jnp.tile(sin, (1, reps)) * _rope_sign()


def _rope_tables_range(start, n, block=256):
    assert n % block == 0
    reps = 2 * LANES // HEAD_DIM
    cos_a, sin_a = _cos_sin(start + block * jnp.arange(n // block))
    cos_b, sin_b = _cos_sin(jnp.arange(block))
    ca, sa = jnp.tile(cos_a, (1, reps))[:, None, :], jnp.tile(sin_a, (1, reps))[:, None, :]
    cb, sb = jnp.tile(cos_b, (1, reps))[None], jnp.tile(sin_b, (1, reps))[None]
    cos = (ca * cb - sa * sb).reshape(n, LANES)
    sin = ((sa * cb + ca * sb) * _rope_sign()).reshape(n, LANES)
    return cos, sin


def _prompt_attn_kernel(qt_ref, k_ref, vt_ref, km_ref, vtm_ref,
                        lq1_ref, lk1_ref, lq2_ref, lk2_ref, sg_ref, o_ref,
                        m_ref, acc_ref, sa_ref, sb_ref):
    i = pl.program_id(1)
    t = ATTN_T
    qt = jnp.concatenate([qt_ref[0, 0], qt_ref[0, 1]], axis=1)

    def scores(c):
        return _dot(k_ref[0, pl.ds(pl.multiple_of(c * t, t), t), :], qt)

    def values(c):
        return vt_ref[0, :, pl.ds(pl.multiple_of(c * t, t), t)]

    def update(s, vt):
        m_prev = m_ref[...]
        m_new = jnp.maximum(m_prev, jnp.max(s, axis=0, keepdims=True))
        alpha = jnp.exp2(m_prev - m_new)
        p = jnp.exp2(s - m_new).astype(BF16)
        acc_ref[...] = alpha * acc_ref[...] + _dot(vt, p)
        m_ref[...] = m_new

    m_ref[...] = jnp.full(m_ref.shape, NEG_BIG, F32)
    acc_ref[...] = jnp.zeros(acc_ref.shape, F32)

    key = lax.broadcasted_iota(jnp.int32, (LANES, 2 * t), 0)
    s_meta = jnp.where(key < N_META, _dot(km_ref[0], qt), NEG_BIG)
    sa_ref[...] = scores(0)
    update(s_meta, vtm_ref[0])

    def run(first, count, prefetch=None, diagonal_last=False):
        bufs = (sa_ref, sb_ref)
        for n in range(count):
            cur, nxt = bufs[n % 2], bufs[(n + 1) % 2]
            if n + 1 < count:
                nxt[...] = scores(first + n + 1)
            elif prefetch is not None:
                nxt[...] = scores(prefetch)
            s = cur[...]
            if diagonal_last and n == count - 1:
                key = lax.broadcasted_iota(jnp.int32, (t, 2 * t), 0)
                qry = lax.broadcasted_iota(jnp.int32, (t, 2 * t), 1) % t
                s = jnp.where(key <= qry, s, NEG_BIG)
            update(s, values(first + n))

    group = ATTN_GROUP

    def body(n, carry):
        run(group * n, group, prefetch=group * n + group)
        return carry

    lax.fori_loop(0, i // group, body, 0)

    for rest in range(group):
        @pl.when(i % group == rest)
        def _():
            run(i - rest, rest + 1, diagonal_last=True)

    lam = _lam(lq1_ref[...], lk1_ref[...], lq2_ref[...], lk2_ref[...])
    o = acc_ref[:V_DIM, :] * (1.0 / acc_ref[V_DIM:V_DIM + 1, :])
    o = o[:, :t] - lam * o[:, t:]
    o = o * lax.rsqrt(jnp.mean(o * o, axis=0, keepdims=True) + SUBLN_EPS) * sg_ref[...]
    o_ref[...] = (o * (1.0 - _lam_init(0))).T.astype(o_ref.dtype)


def _prompt_attention(qt, kb, vt, km, vtm, lams, subln_gain_col):
    rows = kb.shape[1]
    t = ATTN_T
    assert rows % t == 0
    vec = lambda n: pl.BlockSpec((1, n), lambda h, i: (0, 0))
    head = lambda *shape: pl.BlockSpec((1,) + shape, lambda h, i: (h,) + (0,) * len(shape))
    return pl.pallas_call(
        _prompt_attn_kernel,
        grid=(N_HEADS, rows // t),
        in_specs=[pl.BlockSpec((1, 2, V_DIM, t), lambda h, i: (h, 0, 0, i)),
                  head(rows, V_DIM), head(VT_ROWS, rows), head(LANES, V_DIM), head(VT_ROWS, LANES),
                  vec(HEAD_DIM), vec(HEAD_DIM), vec(HEAD_DIM), vec(HEAD_DIM),
                  pl.BlockSpec((V_DIM, 1), lambda h, i: (0, 0))],
        out_specs=pl.BlockSpec((t, V_DIM), lambda h, i: (i, h)),
        out_shape=jax.ShapeDtypeStruct((rows, D_ATTN), BF16),
        scratch_shapes=[pltpu.VMEM((1, 2 * t), F32),
                        pltpu.VMEM((VT_ROWS, 2 * t), F32),
                        pltpu.VMEM((t, 2 * t), F32),
                        pltpu.VMEM((t, 2 * t), F32)],
        compiler_params=pltpu.CompilerParams(
            dimension_semantics=("arbitrary", "arbitrary"), vmem_limit_bytes=VMEM_LIMIT),
        name="prompt_attention",
    )(qt, kb, vt, km, vtm, *lams, subln_gain_col)


def _decode_attn_kernel(pt_ref, q_ref, kn_ref, vn_ref, *rest):
    n = PAGES_PER_STEP
    k_refs = rest[:n]
    v_refs = rest[n:2 * n]
    lq1_ref, lk1_ref, lq2_ref, lk2_ref, sg_ref, o_ref, m_ref, l_ref, acc_ref = rest[2 * n:]
    step = pl.program_id(1)
    q = q_ref[0]
    n_rows = q.shape[0]
    n_new = n_rows // (2 * N_HEADS)

    def own_head(n_keys):
        row = lax.broadcasted_iota(jnp.int32, (n_rows, n_keys), 0)
        col = lax.broadcasted_iota(jnp.int32, (n_rows, n_keys), 1)
        return row, col, (col % N_HEADS) == (row // (2 * n_new))

    @pl.when(step == 0)
    def _():
        row, col, own = own_head(kn_ref.shape[1])
        s = jnp.where(own & ((col // N_HEADS) <= (row % n_new)), _dot_nt(q, kn_ref[0]), NEG_BIG)
        m = jnp.max(s, axis=1, keepdims=True)
        p = jnp.exp2(s - m)
        m_ref[...] = jnp.broadcast_to(m, m_ref.shape)
        l_ref[...] = jnp.broadcast_to(jnp.sum(p, axis=1, keepdims=True), l_ref.shape)
        acc_ref[...] = _dot(p.astype(BF16), vn_ref[0])

    _, _, own = own_head(PAGE_SIZE * N_HEADS)
    scores = [jnp.where(own, _dot_nt(q, k_refs[r][0].astype(BF16)), NEG_BIG) for r in range(n)]
    m_prev = m_ref[...]
    l = l_ref[...]
    acc = acc_ref[...]
    for g in range(0, n, PAGE_GROUP):
        m_cur = scores[g]
        for s in scores[g + 1:g + PAGE_GROUP]:
            m_cur = jnp.maximum(m_cur, s)
        m_new = jnp.maximum(m_prev, jnp.max(m_cur, axis=1, keepdims=True))
        alpha = jnp.exp2(m_prev - m_new)
        l = alpha * l
        acc = alpha * acc
        for r in range(g, g + PAGE_GROUP):
            p = jnp.exp2(scores[r] - m_new[:, :1])
            l = l + jnp.sum(p, axis=1, keepdims=True)
            acc = acc + _dot(p.astype(BF16), v_refs[r][0].astype(BF16))
        m_prev = m_new
    m_ref[...] = m_prev
    l_ref[...] = l
    acc_ref[...] = acc

    @pl.when(step == pl.num_programs(1) - 1)
    def _():
        lam = _lam(lq1_ref[...], lk1_ref[...], lq2_ref[...], lk2_ref[...])
        o = acc_ref[...] / l_ref[...]
        o = o - lam * pltpu.roll(o, n_rows - n_new, axis=0)
        o = _rms(o, sg_ref[...], SUBLN_EPS) * (1.0 - _lam_init(0))
        for h in range(N_HEADS):
            o_ref[0, :, h * V_DIM:(h + 1) * V_DIM] = o[h * 2 * n_new:h * 2 * n_new + n_new]


def _decode_attention(page_table, q_seq, k_new, v_new, cache_k, cache_v, lams, subln_gain):
    n_seq, n_pages = page_table.shape
    n_rows = q_seq.shape[1]
    n_new = n_rows // (2 * N_HEADS)
    page_rows = PAGE_SIZE * N_HEADS
    assert n_pages % PAGES_PER_STEP == 0 and cache_k.shape[1:] == (page_rows, V_DIM)
    page_spec = lambda r: pl.BlockSpec(
        (1, page_rows, V_DIM), lambda b, s, pt: (pt[b, s * PAGES_PER_STEP + r], 0, 0))
    seq_spec = lambda *shape: pl.BlockSpec((1,) + shape, lambda b, s, pt: (b,) + (0,) * len(shape))
    vec = lambda n: pl.BlockSpec((1, n), lambda b, s, pt: (0, 0))
    grid_spec = pltpu.PrefetchScalarGridSpec(
        num_scalar_prefetch=1,
        grid=(n_seq, n_pages // PAGES_PER_STEP),
        in_specs=[seq_spec(n_rows, V_DIM), seq_spec(*k_new.shape[1:]), seq_spec(*v_new.shape[1:])]
                 + [page_spec(r) for r in range(PAGES_PER_STEP)] * 2
                 + [vec(HEAD_DIM)] * 4 + [vec(V_DIM)],
        out_specs=seq_spec(n_new, D_ATTN),
        scratch_shapes=[pltpu.VMEM((n_rows, LANES), F32),
                        pltpu.VMEM((n_rows, LANES), F32),
                        pltpu.VMEM((n_rows, V_DIM), F32)],
    )
    return pl.pallas_call(
        _decode_attn_kernel,
        grid_spec=grid_spec,
        out_shape=jax.ShapeDtypeStruct((n_seq, n_new, D_ATTN), F32),
        compiler_params=pltpu.CompilerParams(
            dimension_semantics=("arbitrary", "arbitrary"), vmem_limit_bytes=VMEM_LIMIT),
        name="decode_attention",
    )(page_table, q_seq, k_new, v_new, *([cache_k] * PAGES_PER_STEP), *([cache_v] * PAGES_PER_STEP),
      *lams, subln_gain)


def _merge_tail(attn, pool_diffs, gate_act, x, wp_ref, ps_ref, wo_ref, fg_ref):
    pool = jnp.concatenate(
        [_dot(pool_diffs[g].astype(BF16), wp_ref[g]) for g in range(len(POOL_WINDOWS))], axis=1)
    mixed = jnp.concatenate([attn.astype(F32), pool * ps_ref[...]], axis=1)
    mixed = mixed * gate_act.astype(F32)
    h = x + _dot(mixed.astype(BF16), wo_ref[...])
    return _rms(h, fg_ref[...], RMS_EPS)


def _prompt_merge_kernel(attn_ref, u_ref, um_ref, gate_ref, x_ref, wp_ref, ps_ref, wo_ref, fg_ref,
                         y_ref, ubuf):
    rows = u_ref.shape[0]

    @pl.when(pl.program_id(0) == 0)
    def _():
        ubuf[0:HALO, :] = um_ref[...]

    ubuf[HALO:HALO + rows, :] = u_ref[...]
    diffs = []
    for g, w in enumerate(POOL_WINDOWS):
        sl = slice(g * POOL_GROUP, (g + 1) * POOL_GROUP)
        total = ubuf[HALO:HALO + rows, sl]
        for back in range(1, w):
            total = total + ubuf[HALO - back:HALO - back + rows, sl]
        diffs.append(total / float(w) - ubuf[HALO:HALO + rows, sl])
    y_ref[...] = _merge_tail(attn_ref[...], diffs, gate_ref[...], x_ref[...],
                             wp_ref, ps_ref, wo_ref, fg_ref)
    ubuf[0:HALO, :] = ubuf[rows:rows + HALO, :]


def _prompt_merge(attn, u, u_meta, gate, x, wp_b, pool_scale, wo_b, final_gain):
    rows = x.shape[0]
    t = MERGE_ROWS
    assert rows % t == 0 and u_meta.shape[0] == HALO
    row_spec = lambda cols: pl.BlockSpec((t, cols), lambda i: (i, 0))
    full = lambda *shape: pl.BlockSpec(shape, lambda i: (0,) * len(shape))
    return pl.pallas_call(
        _prompt_merge_kernel,
        grid=(rows // t,),
        in_specs=[row_spec(D_ATTN), row_spec(D_POOL), full(HALO, D_POOL), row_spec(D_MODEL),
                  row_spec(D_MODEL), full(len(POOL_WINDOWS), POOL_GROUP, POOL_GROUP),
                  full(1, D_POOL), full(D_MODEL, D_MODEL), full(1, D_MODEL)],
        out_specs=row_spec(D_MODEL),
        out_shape=jax.ShapeDtypeStruct((rows, D_MODEL), F32),
        scratch_shapes=[pltpu.VMEM((t + HALO, D_POOL), F32)],
        compiler_params=pltpu.CompilerParams(
            dimension_semantics=("arbitrary",), vmem_limit_bytes=VMEM_LIMIT),
        name="prompt_merge",
    )(attn, u, u_meta, gate, x, wp_b, pool_scale, wo_b, final_gain)


def _sample_merge_kernel(attn_ref, st_ref, un_ref, gate_ref, x_ref, wp_ref, ps_ref, wo_ref, fg_ref, y_ref):
    n_state, n_seq, _ = st_ref.shape
    n_new = un_ref.shape[0] // n_seq

    def ext(t, sl):
        if t < n_state:
            return st_ref[t, :, sl]
        return un_ref[(t - n_state) * n_seq:(t - n_state + 1) * n_seq, sl]

    diffs = []
    for g, w in enumerate(POOL_WINDOWS):
        sl = slice(g * POOL_GROUP, (g + 1) * POOL_GROUP)
        per_token = []
        for j in range(n_new):
            total = ext(n_state + j, sl)
            for back in range(1, w):
                total = total + ext(n_state + j - back, sl)
            per_token.append(total / float(w) - ext(n_state + j, sl))
        diffs.append(jnp.concatenate(per_token, axis=0))
    y_ref[...] = _merge_tail(attn_ref[...], diffs, gate_ref[...], x_ref[...],
                             wp_ref, ps_ref, wo_ref, fg_ref)


def _sample_merge(attn, state_t, u_new, gate, x, wp_b, pool_scale, wo_b, final_gain):
    rows = x.shape[0]
    return pl.pallas_call(
        _sample_merge_kernel,
        out_shape=jax.ShapeDtypeStruct((rows, D_MODEL), F32),
        compiler_params=pltpu.CompilerParams(vmem_limit_bytes=VMEM_LIMIT),
        name="sample_merge",
    )(attn, state_t, u_new, gate, x, wp_b, pool_scale, wo_b, final_gain)


def kernel(x_prompt, x_sample, cache_k, cache_v, state_pool, page_table, meta_tokens, norm_gain, w_in,
           lambda_q1, lambda_k1, lambda_q2, lambda_k2, subln_gain, w_pool, pool_scale, w_out, final_gain):
    batch, seq, _ = x_prompt.shape
    n_seq, n_new, _ = x_sample.shape
    n_samp = n_seq * n_new
    assert batch == 1 and cache_k.shape[0] == 1 and N_META + n_samp <= SMALL_ROWS
    p_len = page_table.shape[1] * PAGE_SIZE
    assert N_META + 1 >= max(POOL_WINDOWS) and p_len - POOL_STATE + 1 >= max(POOL_WINDOWS)

    w_in_b = w_in[0].astype(BF16)
    w_out_b = w_out[0].astype(BF16)
    w_pool_b = w_pool[0].astype(BF16)
    lams = (lambda_q1, lambda_k1, lambda_q2, lambda_k2)
    final_gain2 = final_gain[None, :]

    n_pad = SMALL_ROWS - N_META - n_samp
    pos_small = jnp.concatenate([jnp.arange(N_META), jnp.repeat(p_len + jnp.arange(n_new), n_seq),
                                 jnp.zeros((n_pad,), jnp.int32)])
    x_samp = x_sample.transpose(1, 0, 2).reshape(n_samp, D_MODEL)
    x_small = jnp.concatenate([meta_tokens, x_samp, jnp.zeros((n_pad, D_MODEL), F32)], axis=0)
    qt_s, kf_s, kb_s, vf_s, vt_s, u_s, g_s = _project(
        x_small, _rope_tables(pos_small), norm_gain, w_in_b, SMALL_ROWS)
    samp = slice(N_META, N_META + n_samp)
    n_meta_rows = N_META * N_HEADS
    cache_rows = slice(n_meta_rows, (N_META + n_samp) * N_HEADS)

    qt_p, kf_p, kb_p, vf_p, vt_p, u_p, g_p = _project(
        x_prompt[0], _rope_tables_range(N_META, seq), norm_gain, w_in_b, PROJ_ROWS,
        lead_rows=n_meta_rows)

    km = jnp.pad(kb_s[:, :N_META], ((0, 0), (0, LANES - N_META), (0, 0)))
    vtm = jnp.pad(vt_s[:, :, :N_META], ((0, 0), (0, 0), (0, LANES - N_META)))
    attn_p = _prompt_attention(qt_p, kb_p, vt_p, km, vtm, lams, subln_gain.reshape(V_DIM, 1))
    y_prompt = _prompt_merge(attn_p, u_p, u_s[:N_META], g_p, x_prompt[0], w_pool_b, pool_scale,
                             w_out_b, final_gain2)

    q_seq = qt_s[:, :, :, samp].reshape(N_HEADS, 2, V_DIM, n_new, n_seq)
    q_seq = q_seq.transpose(4, 0, 1, 3, 2).reshape(n_seq, N_HEADS * 2 * n_new, V_DIM)
    by_seq = lambda a: a.reshape(n_new, n_seq, -1).transpose(1, 0, 2)
    pad_new = ((0, 0), (0, LANES - n_new * N_HEADS), (0, 0))
    k_samp = by_seq(kf_s[cache_rows])
    v_samp = by_seq(vf_s[cache_rows])
    k_new = jnp.pad(k_samp.astype(BF16).reshape(n_seq, n_new * N_HEADS, V_DIM), pad_new)
    v_new = jnp.pad(v_samp.astype(BF16).reshape(n_seq, n_new * N_HEADS, V_DIM), pad_new)
    n_pool_pages = cache_k.shape[1]
    attn_s = _decode_attention(page_table, q_seq, k_new, v_new,
                               cache_k.reshape(n_pool_pages, PAGE_SIZE * N_HEADS, V_DIM),
                               cache_v.reshape(n_pool_pages, PAGE_SIZE * N_HEADS, V_DIM), lams, subln_gain)
    y_samp = _sample_merge(attn_s.transpose(1, 0, 2).reshape(n_samp, D_ATTN),
                           state_pool[0].transpose(1, 0, 2), u_s[samp], g_s[samp], x_samp,
                           w_pool_b, pool_scale, w_out_b, final_gain2)

    l_p = seq + N_META
    new_k_prompt = lax.dynamic_update_slice(kf_p, kf_s[:n_meta_rows], (0, 0)).reshape(1, 1, l_p, N_HEADS, V_DIM)
    new_v_prompt = lax.dynamic_update_slice(vf_p, vf_s[:n_meta_rows], (0, 0)).reshape(1, 1, l_p, N_HEADS, V_DIM)
    new_pool_prompt = u_p[-POOL_STATE:].reshape(1, 1, POOL_STATE, D_POOL)
    new_k_sample = k_samp.reshape(1, n_seq, n_new, N_HEADS, V_DIM)
    new_v_sample = v_samp.reshape(1, n_seq, n_new, N_HEADS, V_DIM)
    new_pool_sample = jnp.concatenate([state_pool[0], by_seq(u_s[samp])], axis=1)[None, :, -POOL_STATE:]
    return (y_prompt[None], by_seq(y_samp), new_k_prompt, new_v_prompt,
            new_pool_prompt, new_k_sample, new_v_sample, new_pool_sample)
```
